```python
import math
import jax, jax.numpy as jnp
from jax import lax
import numpy as np

D_MODEL = 4096
BATCH = 4
SEQ = 4096
DEPTH = 1

D_FF = 11008
D_SSM = D_MODEL // 2
SSM_GROUP = 16
N_SSM_GROUPS = D_SSM // SSM_GROUP
SSM_STATE = 64
DT_MIN = 0.001
DT_MAX = 0.1
D_CONV = D_MODEL // 2
CONV_WIDTH = 3
N_MEM = 256
N_XHEADS = 4
XHEAD_DIM = D_MODEL // N_XHEADS
MIX_IN_COLS = D_SSM + 3 * D_CONV + 2 * D_MODEL
MIX_SPLITS = (D_SSM, D_SSM + D_CONV, D_SSM + 2 * D_CONV, D_SSM + 3 * D_CONV,
              D_SSM + 3 * D_CONV + D_MODEL)
RMS_EPS = 1e-6

kernel_name = "hybrid_s5_shortconv_gated_macaron_xattn"


def rms_norm(x, g):
    xf = x.astype(jnp.float32)
    y = xf * lax.rsqrt(jnp.mean(xf * xf, axis=-1, keepdims=True) + RMS_EPS)
    return (y * g.astype(jnp.float32)).astype(x.dtype)


def swiglu(h, w_in, w_out):
    a, b = jnp.split(h @ w_in, 2, axis=-1)
    return (jax.nn.silu(a) * b) @ w_out


def _ssm_combine(e1, e2):
    a1, b1 = e1
    a2, b2 = e2
    return a1 * a2, a2 * b1 + b2


def s5_mixer(u, a_re, a_im, log_dt, b_re, b_im, c_re, c_im, d_skip):
    bsz, seq, _ = u.shape
    f32 = jnp.float32
    uf = u.astype(f32).reshape(bsz, seq, N_SSM_GROUPS, SSM_GROUP)
    lam = lax.complex(a_re.astype(f32), a_im.astype(f32))
    dt = jnp.exp(log_dt.astype(f32))[:, None]
    lam_bar = jnp.exp(lam * dt)
    b = lax.complex(b_re.astype(f32), b_im.astype(f32))
    b_bar = ((lam_bar - 1.0) / lam)[..., None] * b
    c = lax.complex(c_re.astype(f32), c_im.astype(f32))
    bu = jnp.einsum('blgh,gph->blgp', uf, b_bar)
    a_elems = jnp.broadcast_to(lam_bar, (1, seq) + lam_bar.shape)
    _, states = lax.associative_scan(_ssm_combine, (a_elems, bu), axis=1)
    y = jnp.einsum('blgp,ghp->blgh', states, c).real
    y = y.reshape(bsz, seq, D_SSM) + d_skip.astype(f32) * uf.reshape(bsz, seq, D_SSM)
    return y.astype(u.dtype)


def causal_depthwise_conv(z, w):
    k, c = w.shape
    rhs = w.astype(z.dtype)[:, None, :]
    return lax.conv_general_dilated(z, rhs, window_strides=(1,), padding=((k - 1, 0),),
                                    dimension_numbers=('NWC', 'WIO', 'NWC'),
                                    feature_group_count=c)


def cross_attention(hq, mkv, wq, wk, wv, wo):
    bsz, seq, _ = hq.shape
    n_mem = mkv.shape[1]
    q = (hq @ wq).reshape(bsz, seq, N_XHEADS, XHEAD_DIM)
    k = (mkv @ wk).reshape(bsz, n_mem, N_XHEADS, XHEAD_DIM)
    v = (mkv @ wv).reshape(bsz, n_mem, N_XHEADS, XHEAD_DIM)
    s = jnp.einsum('blhd,bmhd->bhlm', q, k).astype(jnp.float32) * (XHEAD_DIM ** -0.5)
    p = jax.nn.softmax(s, axis=-1).astype(v.dtype)
    o = jnp.einsum('bhlm,bmhd->blhd', p, v).reshape(bsz, seq, D_MODEL)
    return o @ wo


def setup_inputs(seed: int = 0) -> dict:
    key = jax.random.key(seed)
    ks = iter(jax.random.split(key, 40))
    f32 = jnp.float32

    def nrm(shape, scale):
        return jax.random.normal(next(ks), shape, f32) * scale

    def gain(shape):
        return 1.0 + 0.02 * jax.random.normal(next(ks), shape, f32)

    L, G, P, H = DEPTH, N_SSM_GROUPS, SSM_STATE, SSM_GROUP
    a_im_base = math.pi * jnp.arange(P, dtype=f32)
    return {
        "x": nrm((BATCH, SEQ, D_MODEL), 1.0),
        "mem": nrm((BATCH, N_MEM, D_MODEL), 1.0),
        "ffn1_norm": gain((L, D_MODEL)),
        "ffn1_w_in": nrm((L, D_MODEL, 2 * D_FF), D_MODEL ** -0.5),
        "ffn1_w_out": nrm((L, D_FF, D_MODEL), D_FF ** -0.5),
        "mix_norm": gain((L, D_MODEL)),
        "mix_w_in": nrm((L, D_MODEL, MIX_IN_COLS), D_MODEL ** -0.5),
        "ssm_a_re": -0.5 + nrm((L, G, P), 0.01),
        "ssm_a_im": a_im_base + nrm((L, G, P), 0.01),
        "ssm_log_dt": jax.random.uniform(next(ks), (L, G), f32,
                                         math.log(DT_MIN), math.log(DT_MAX)),
        "ssm_b_re": nrm((L, G, P, H), (2 * H) ** -0.5),
        "ssm_b_im": nrm((L, G, P, H), (2 * H) ** -0.5),
        "ssm_c_re": nrm((L, G, H, P), 1.0),
        "ssm_c_im": nrm((L, G, H, P), 1.0),
        "ssm_d": nrm((L, D_SSM), 1.0),
        "ssm_glu_w": nrm((L, D_SSM, 2 * D_MODEL), D_SSM ** -0.5),
        "conv_w": nrm((L, CONV_WIDTH, D_CONV), CONV_WIDTH ** -0.5),
        "conv_w_out": nrm((L, D_CONV, D_MODEL), D_CONV ** -0.5),
        "mix_w_out": nrm((L, D_MODEL, D_MODEL), D_MODEL ** -0.5),
        "xattn_norm": gain((L, D_MODEL)),
        "mem_norm": gain((L, D_MODEL)),
        "xattn_wq": nrm((L, D_MODEL, D_MODEL), D_MODEL ** -0.5),
        "xattn_wk": nrm((L, D_MODEL, D_MODEL), D_MODEL ** -0.5),
        "xattn_wv": nrm((L, D_MODEL, D_MODEL), D_MODEL ** -0.5),
        "xattn_wo": nrm((L, D_MODEL, D_MODEL), D_MODEL ** -0.5),
        "ffn2_norm": gain((L, D_MODEL)),
        "ffn2_w_in": nrm((L, D_MODEL, 2 * D_FF), D_MODEL ** -0.5),
        "ffn2_w_out": nrm((L, D_FF, D_MODEL), D_FF ** -0.5),
        "final_norm": gain((D_MODEL,)),
    }


def reference(x, mem, ffn1_norm, ffn1_w_in, ffn1_w_out, mix_norm, mix_w_in,
              ssm_a_re, ssm_a_im, ssm_log_dt, ssm_b_re, ssm_b_im, ssm_c_re, ssm_c_im,
              ssm_d, ssm_glu_w, conv_w, conv_w_out, mix_w_out,
              xattn_norm, mem_norm, xattn_wq, xattn_wk, xattn_wv, xattn_wo,
              ffn2_norm, ffn2_w_in, ffn2_w_out, final_norm):
    h = x
    for l in range(DEPTH):
        h = h + 0.5 * swiglu(rms_norm(h, ffn1_norm[l]), ffn1_w_in[l], ffn1_w_out[l])

        u = rms_norm(h, mix_norm[l])
        u_ssm, cb, cc, ch, g_a, g_b = jnp.split(u @ mix_w_in[l], MIX_SPLITS, axis=-1)

        ys = jax.nn.gelu(s5_mixer(u_ssm, ssm_a_re[l], ssm_a_im[l], ssm_log_dt[l],
                                  ssm_b_re[l], ssm_b_im[l], ssm_c_re[l], ssm_c_im[l],
                                  ssm_d[l]), approximate=False)
        val, gl = jnp.split(ys @ ssm_glu_w[l], 2, axis=-1)
        y_a = val * jax.nn.sigmoid(gl)

        y_b = (cb * causal_depthwise_conv(cc * ch, conv_w[l])) @ conv_w_out[l]

        merged = jax.nn.sigmoid(g_a) * y_a + jax.nn.sigmoid(g_b) * y_b
        h = h + merged @ mix_w_out[l]

        h = h + cross_attention(rms_norm(h, xattn_norm[l]), rms_norm(mem, mem_norm[l]),
                                xattn_wq[l], xattn_wk[l], xattn_wv[l], xattn_wo[l])

        h = h + 0.5 * swiglu(rms_norm(h, ffn2_norm[l]), ffn2_w_in[l], ffn2_w_out[l])
    return rms_norm(h, final_norm)
```

```python
import functools

import jax
import jax.numpy as jnp
from jax import lax
from jax.experimental import pallas as pl
from jax.experimental.pallas import tpu as pltpu

F32 = jnp.float32
BF16 = jnp.bfloat16
RMS_EPS = 1e-6

V7X_LANES = 128
V7X_MXU_DIM = 256
V7X_VMEM_BYTES = 64 * 1024 * 1024
VMEM_LIMIT = V7X_VMEM_BYTES - 8 * 1024 * 1024

SSM_CHUNK = 16
SSM_GROUPS_PER_BLOCK = 16
SSM_SUB = 4


def _params(dims):
    return pltpu.CompilerParams(dimension_semantics=dims, vmem_limit_bytes=VMEM_LIMIT)


def _blk(dim, pref):
    b = min(dim, pref)
    assert dim % b == 0, (dim, pref)
    return b


def _rms_rows(x, g):
    ms = jnp.mean(x * x, axis=-1, keepdims=True)
    return x * lax.rsqrt(ms + RMS_EPS) * g


def _row_chunks(n_rows, chunk, body):
    chunk = min(chunk, n_rows)
    assert n_rows % chunk == 0

    def step(c, carry):
        body(pl.ds(pl.multiple_of(c * chunk, chunk), chunk))
        return carry

    lax.fori_loop(0, n_rows // chunk, step, 0)


def _ffn_kernel(x_ref, g_ref, win_ref, wout_ref, g2_ref, o_ref, xn_ref, *, bf, final_norm):
    f = pl.program_id(1)
    bm = x_ref.shape[0]

    @pl.when(f == 0)
    def _():
        def body(rows):
            xn_ref[rows, :] = _rms_rows(x_ref[rows, :], g_ref[...]).astype(BF16)
            o_ref[rows, :] = jnp.zeros((rows.size, o_ref.shape[1]), F32)

        _row_chunks(bm, 64, body)

    ab = jnp.dot(xn_ref[...], win_ref[...], preferred_element_type=F32)
    a = ab[:, :bf]
    b = ab[:, bf:]
    gated = (a * jax.nn.sigmoid(a) * b).astype(BF16)
    d = o_ref.shape[1]
    bn = min(d, 1024)
    for n0 in range(0, d, bn):
        o_ref[:, n0:n0 + bn] += jnp.dot(gated, wout_ref[:, n0:n0 + bn], preferred_element_type=F32)

    @pl.when(f == pl.num_programs(1) - 1)
    def _():
        def body(rows):
            h = x_ref[rows, :] + 0.5 * o_ref[rows, :]
            if final_norm:
                h = _rms_rows(h, g2_ref[...])
            o_ref[rows, :] = h

        _row_chunks(bm, 64, body)


def _ffn(x, g, w_in, w_out, g2, *, final_norm, name):
    t, d = x.shape
    f_dim = w_out.shape[0]
    bf = _blk(f_dim, V7X_MXU_DIM)
    nf = f_dim // bf
    bm = _blk(t, 512)
    w_in_r = w_in.reshape(d, 2, nf, bf).transpose(0, 2, 1, 3).reshape(d, 2 * f_dim).astype(BF16)
    return pl.pallas_call(
        functools.partial(_ffn_kernel, bf=bf, final_norm=final_norm),
        grid=(t // bm, nf),
        in_specs=[
            pl.BlockSpec((bm, d), lambda i, f: (i, 0)),
            pl.BlockSpec((1, d), lambda i, f: (0, 0)),
            pl.BlockSpec((d, 2 * bf), lambda i, f: (0, f)),
            pl.BlockSpec((bf, d), lambda i, f: (f, 0)),
            pl.BlockSpec((1, d), lambda i, f: (0, 0)),
        ],
        out_specs=pl.BlockSpec((bm, d), lambda i, f: (i, 0)),
        out_shape=jax.ShapeDtypeStruct((t, d), F32),
        scratch_shapes=[pltpu.VMEM((bm, d), BF16)],
        compiler_params=_params(("parallel", "arbitrary")),
        name=name,
    )(x, g.reshape(1, d), w_in_r, w_out.astype(BF16), g2.reshape(1, d))


def _norm_mm_kernel(a_ref, g_ref, w_ref, o_ref, an_ref):
    @pl.when(pl.program_id(1) == 0)
    def _():
        def body(rows):
            an_ref[rows, :] = _rms_rows(a_ref[rows, :], g_ref[...]).astype(BF16)

        _row_chunks(a_ref.shape[0], 64, body)

    o_ref[...] = jnp.dot(an_ref[...], w_ref[...], preferred_element_type=F32).astype(o_ref.dtype)


def _norm_mm(a, g, w, *, name, bm=512, bn=1024):
    m, k = a.shape
    n = w.shape[1]
    bm = _blk(m, bm)
    bn = _blk(n, bn)
    return pl.pallas_call(
        _norm_mm_kernel,
        grid=(m // bm, n // bn),
        in_specs=[
            pl.BlockSpec((bm, k), lambda i, j: (i, 0)),
            pl.BlockSpec((1, k), lambda i, j: (0, 0)),
            pl.BlockSpec((k, bn), lambda i, j: (0, j)),
        ],
        out_specs=pl.BlockSpec((bm, bn), lambda i, j: (i, j)),
        out_shape=jax.ShapeDtypeStruct((m, n), BF16),
        scratch_shapes=[pltpu.VMEM((bm, k), BF16)],
        compiler_params=_params(("parallel", "arbitrary")),
        name=name,
    )(a, g.reshape(1, k), w.astype(BF16))


def _mm_res_kernel(a_ref, w_ref, res_ref, o_ref):
    o_ref[...] = res_ref[...] + jnp.dot(a_ref[...], w_ref[...], preferred_element_type=F32)


def _mm_res(a, w, res, *, name, bm=1024, bn=1024):
    m, k = a.shape
    n = w.shape[1]
    bm = _blk(m, bm)
    bn = _blk(n, bn)
    return pl.pallas_call(
        _mm_res_kernel,
        grid=(m // bm, n // bn),
        in_specs=[
            pl.BlockSpec((bm, k), lambda i, j: (i, 0)),
            pl.BlockSpec((k, bn), lambda i, j: (0, j)),
            pl.BlockSpec((bm, bn), lambda i, j: (i, j)),
        ],
        out_specs=pl.BlockSpec((bm, bn), lambda i, j: (i, j)),
        out_shape=jax.ShapeDtypeStruct((m, n), F32),
        compiler_params=_params(("parallel", "arbitrary")),
        name=name,
    )(a, w.astype(BF16), res)


def _cmul(ar, ai, br, bi):
    return ar * br - ai * bi, ar * bi + ai * br


def _gelu_exact(x):
    return 0.5 * x * (1.0 + lax.erf(x * (2.0 ** -0.5)))


def _ssm_kernel(u_ref, are_ref, aim_ref, ldt_ref, braw_ref, craw_ref, dsk_ref, o_ref,
                bstack_ref, nst_ref, dstack_ref, pw_ref, apow_ref, sre_ref, sim_ref, w_ref,
                *, n_seq, n_chunks, n_log):
    w256 = V7X_MXU_DIM
    ns = braw_ref.shape[2] // 2
    n_strip = ns // V7X_LANES
    off = sre_ref.shape[2] - n_chunks
    rows = n_seq * n_chunks

    @pl.when(pl.program_id(1) == 0)
    def _prepare():
        ar = are_ref[0]
        ai = aim_ref[0]
        dt = jnp.exp(ldt_ref[0])
        mag = jnp.exp(ar * dt)
        lr = mag * jnp.cos(ai * dt)
        li = mag * jnp.sin(ai * dt)
        den = ar * ar + ai * ai
        zr = ((lr - 1.0) * ar + li * ai) / den
        zi = (li * ar - (lr - 1.0) * ai) / den
        sr, si = _cmul(braw_ref[0, :, :ns], braw_ref[0, :, ns:], zr, zi)
        cr = craw_ref[0, :, :ns]
        ci = craw_ref[0, :, ns:]
        cn = jnp.concatenate([cr, -ci], axis=1).astype(BF16)
        for j in range(SSM_CHUNK):
            bs = jnp.concatenate([sr, si], axis=1).astype(BF16)
            dj = lax.dot_general(bs, cn, (((1,), (1,)), ((), ())), preferred_element_type=F32)
            dstack_ref[(SSM_CHUNK - 1 - j) * w256:(SSM_CHUNK - j) * w256, :] = dj.astype(BF16)
            if j < SSM_SUB:
                bstack_ref[(SSM_SUB - 1 - j) * w256:(SSM_SUB - j) * w256, :] = bs
            sr, si = _cmul(sr, si, lr, li)
        mr, mi = lr, li
        for s in range(SSM_SUB):
            er, ei = _cmul(cr, ci, mr, mi)
            nst_ref[s * w256:(s + 1) * w256, :] = jnp.concatenate([er, -ei], axis=1).astype(BF16)
            if s < SSM_SUB - 1:
                mr, mi = _cmul(mr, mi, lr, li)
        pw_ref[0:1, :] = mr
        pw_ref[1:2, :] = mi
        p8r, p8i = _cmul(mr, mi, mr, mi)
        kr, ki = _cmul(p8r, p8i, p8r, p8i)
        for k in range(n_log):
            for s in range(n_strip):
                apow_ref[k, s, 0:1, :] = kr[:, s * V7X_LANES:(s + 1) * V7X_LANES]
                apow_ref[k, s, 1:2, :] = ki[:, s * V7X_LANES:(s + 1) * V7X_LANES]
            kr, ki = _cmul(kr, ki, kr, ki)
        zeros = jnp.zeros((off, V7X_LANES), F32)
        for s in range(n_strip):
            for q in range(n_seq):
                sre_ref[s, q, 0:off, :] = zeros
                sim_ref[s, q, 0:off, :] = zeros

    p4r = pw_ref[0:1, :]
    p4i = pw_ref[1:2, :]

    xr = xi = None
    for q in range(SSM_SUB):
        z = None
        for s in range(SSM_SUB):
            d = jnp.dot(u_ref[SSM_SUB * q + s], bstack_ref[s * w256:(s + 1) * w256, :],
                        preferred_element_type=F32)
            z = d if z is None else z + d
        if xr is None:
            xr, xi = z[:, :ns], z[:, ns:]
        else:
            xr, xi = _cmul(xr, xi, p4r, p4i)
            xr, xi = xr + z[:, :ns], xi + z[:, ns:]

    for s in range(n_strip):
        lanes = slice(s * V7X_LANES, (s + 1) * V7X_LANES)
        for q in range(n_seq):
            sre_ref[s, q, off:, :] = xr[q * n_chunks:(q + 1) * n_chunks, lanes]
            sim_ref[s, q, off:, :] = xi[q * n_chunks:(q + 1) * n_chunks, lanes]
    for s in range(n_strip):
        for q in range(n_seq):
            for k in range(n_log):
                sh = 1 << k
                kr = apow_ref[k, s, 0:1, :]
                ki = apow_ref[k, s, 1:2, :]
                pr = sre_ref[s, q, off - sh:off - sh + n_chunks, :]
                pi = sim_ref[s, q, off - sh:off - sh + n_chunks, :]
                tr, ti = _cmul(pr, pi, kr, ki)
                sre_ref[s, q, off:, :] = sre_ref[s, q, off:, :] + tr
                sim_ref[s, q, off:, :] = sim_ref[s, q, off:, :] + ti

    ysts = []
    for q in range(SSM_SUB):
        for s in range(n_strip):
            lanes_r = slice(s * V7X_LANES, (s + 1) * V7X_LANES)
            lanes_i = slice(ns + s * V7X_LANES, ns + (s + 1) * V7X_LANES)
            for sq in range(n_seq):
                rs = slice(sq * n_chunks, (sq + 1) * n_chunks)
                if q == 0:
                    vr = sre_ref[s, sq, off - 1:off - 1 + n_chunks, :]
                    vi = sim_ref[s, sq, off - 1:off - 1 + n_chunks, :]
                else:
                    vr, vi = _cmul(sre_ref[s, sq, off:, :], sim_ref[s, sq, off:, :],
                                   p4r[:, lanes_r], p4i[:, lanes_r])
                if q < SSM_SUB - 1:
                    sre_ref[s, sq, off:, :] = vr
                    sim_ref[s, sq, off:, :] = vi
                w_ref[rs, lanes_r] = vr.astype(BF16)
                w_ref[rs, lanes_i] = vi.astype(BF16)
        ysts.append(lax.dot_general(w_ref[...], nst_ref[...], (((1,), (1,)), ((), ())),
                                    preferred_element_type=F32))

    dsk = dsk_ref[0]
    for t in range(SSM_CHUNK):
        y = ysts[t // SSM_SUB][:, (t % SSM_SUB) * w256:(t % SSM_SUB + 1) * w256]
        y = y + dsk * u_ref[t].astype(F32)
        for r in range(t + 1):
            blk = SSM_CHUNK - 1 - t + r
            y = y + jnp.dot(u_ref[r], dstack_ref[blk * w256:(blk + 1) * w256, :],
                            preferred_element_type=F32)
        o_ref[t] = _gelu_exact(y).astype(o_ref.dtype)
    del rows


def _ssm(u_t, a_re, a_im, log_dt, b_re, b_im, c_re, c_im, d_skip, *, n_chunks, name):
    n_pos, n_rows, d_ssm = u_t.shape
    n_groups, n_state = a_re.shape
    h = d_ssm // n_groups
    gpb = SSM_GROUPS_PER_BLOCK
    assert n_pos == SSM_CHUNK and gpb * h == V7X_MXU_DIM and n_groups % gpb == 0
    n_blocks = n_groups // gpb
    ns = gpb * n_state
    n_seq = 1
    n_log = max(1, (n_chunks - 1).bit_length())
    off = max(8, 1 << (n_log - 1))
    eye = jnp.eye(gpb, dtype=F32)

    def block_diag(x):
        x = x.reshape(n_blocks, gpb, h, n_state).transpose(0, 2, 1, 3)
        x = eye[None, :, None, :, None] * x[:, None, :, :, :]
        return x.reshape(n_blocks, gpb * h, ns)

    braw = jnp.concatenate([block_diag(b_re.transpose(0, 2, 1)), block_diag(b_im.transpose(0, 2, 1))], axis=2)
    craw = jnp.concatenate([block_diag(c_re), block_diag(c_im)], axis=2)
    vec = lambda x: x.reshape(n_blocks, 1, ns)
    ldt = jnp.broadcast_to(log_dt[:, None], (n_groups, n_state))
    rows = n_seq * n_chunks
    kernel = functools.partial(_ssm_kernel, n_seq=n_seq, n_chunks=n_chunks, n_log=n_log)
    vspec = pl.BlockSpec((1, 1, ns), lambda g, r: (g, 0, 0))
    mspec = pl.BlockSpec((1, V7X_MXU_DIM, 2 * ns), lambda g, r: (g, 0, 0))
    uspec = pl.BlockSpec((SSM_CHUNK, rows, V7X_MXU_DIM), lambda g, r: (0, r, g))
    return pl.pallas_call(
        kernel,
        grid=(n_blocks, n_rows // rows),
        in_specs=[uspec, vspec, vspec, vspec, mspec, mspec,
                  pl.BlockSpec((1, 1, V7X_MXU_DIM), lambda g, r: (g, 0, 0))],
        out_specs=uspec,
        out_shape=jax.ShapeDtypeStruct(u_t.shape, BF16),
        scratch_shapes=[
            pltpu.VMEM((SSM_SUB * V7X_MXU_DIM, 2 * ns), BF16),
            pltpu.VMEM((SSM_SUB * V7X_MXU_DIM, 2 * ns), BF16),
            pltpu.VMEM((SSM_CHUNK * V7X_MXU_DIM, V7X_MXU_DIM), BF16),
            pltpu.VMEM((8, ns), F32),
            pltpu.VMEM((n_log, ns // V7X_LANES, 8, V7X_LANES), F32),
            pltpu.VMEM((ns // V7X_LANES, n_seq, off + n_chunks, V7X_LANES), F32),
            pltpu.VMEM((ns // V7X_LANES, n_seq, off + n_chunks, V7X_LANES), F32),
            pltpu.VMEM((rows, 2 * ns), BF16),
        ],
        compiler_params=_params(("parallel", "arbitrary")),
        name=name,
    )(u_t, vec(a_re), vec(a_im), vec(ldt), braw, craw, d_skip.reshape(n_blocks, 1, V7X_MXU_DIM))


def _conv_kernel(cb_ref, cc_ref, ch_ref, w_ref, o_ref, z_ref, *, chunk):
    seq = cb_ref.shape[0]
    pad = z_ref.shape[0] - seq
    kw = w_ref.shape[0]
    z_ref[0:pad, :] = jnp.zeros((pad, z_ref.shape[1]), F32)

    def fill(rows):
        z_ref[pl.ds(rows.start + pad, rows.size), :] = cc_ref[rows, :].astype(F32) * ch_ref[rows, :].astype(F32)

    _row_chunks(seq, chunk, fill)
    for c in range(seq // chunk):
        r0 = c * chunk
        acc = None
        for k in range(kw):
            lag = kw - 1 - k
            term = w_ref[k:k + 1, :] * z_ref[pad + r0 - lag:pad + r0 - lag + chunk, :]
            acc = term if acc is None else acc + term
        o_ref[r0:r0 + chunk, :] = (cb_ref[r0:r0 + chunk, :].astype(F32) * acc).astype(o_ref.dtype)


def _conv(proj, conv_w, *, batch, seq, d_conv, col0, name):
    bc = _blk(d_conv, V7X_MXU_DIM)
    nb = d_conv // bc
    chunk = min(seq, 512)
    kw = conv_w.shape[0]
    cspec = lambda k: pl.BlockSpec((seq, bc), lambda b, j: (b, (col0 + k * d_conv) // bc + j))
    return pl.pallas_call(
        functools.partial(_conv_kernel, chunk=chunk),
        grid=(batch, nb),
        in_specs=[cspec(0), cspec(1), cspec(2), pl.BlockSpec((kw, bc), lambda b, j: (0, j))],
        out_specs=pl.BlockSpec((seq, bc), lambda b, j: (b, j)),
        out_shape=jax.ShapeDtypeStruct((batch * seq, d_conv), BF16),
        scratch_shapes=[pltpu.VMEM((8 + seq, bc), F32)],
        compiler_params=_params(("parallel", "parallel")),
        name=name,
    )(proj, proj, proj, conv_w)


def _merge_kernel(ys_ref, cz_ref, wv_ref, wg_ref, wc_ref, ga_ref, gb_ref, o_ref):
    ys = ys_ref[...]
    val = jnp.dot(ys, wv_ref[...], preferred_element_type=F32)
    gate = jnp.dot(ys, wg_ref[...], preferred_element_type=F32)
    yb = jnp.dot(cz_ref[...], wc_ref[...], preferred_element_type=F32)
    ya = val * jax.nn.sigmoid(gate)
    out = jax.nn.sigmoid(ga_ref[...].astype(F32)) * ya + jax.nn.sigmoid(gb_ref[...].astype(F32)) * yb
    o_ref[...] = out.astype(o_ref.dtype)


def _merge(ys, cz, glu_w, conv_w_out, proj, *, d_model, gate_col0, name, bm=1024, bn=512):
    t, k = ys.shape
    bm = _blk(t, bm)
    bn = _blk(d_model, bn)
    nj = d_model // bn
    aspec = pl.BlockSpec((bm, k), lambda i, j: (i, 0))
    wspec = lambda off: pl.BlockSpec((k, bn), lambda i, j: (0, off + j))
    gspec = lambda off: pl.BlockSpec((bm, bn), lambda i, j: (i, off + j))
    glu_w = glu_w.astype(BF16)
    return pl.pallas_call(
        _merge_kernel,
        grid=(t // bm, nj),
        in_specs=[aspec, aspec, wspec(0), wspec(nj), wspec(0),
                  gspec(gate_col0 // bn), gspec(gate_col0 // bn + nj)],
        out_specs=pl.BlockSpec((bm, bn), lambda i, j: (i, j)),
        out_shape=jax.ShapeDtypeStruct((t, d_model), BF16),
        compiler_params=_params(("parallel", "arbitrary")),
        name=name,
    )(ys, cz, glu_w, glu_w, conv_w_out.astype(BF16), proj, proj)


def _attn_kernel(q_ref, k_ref, v_ref, o_ref, *, scale):
    s = lax.dot_general(q_ref[...], k_ref[...], (((1,), (1,)), ((), ())), preferred_element_type=F32) * scale
    p = jnp.exp(s - jnp.max(s, axis=-1, keepdims=True))
    denom = jnp.sum(p, axis=-1, keepdims=True)
    o = jnp.dot(p.astype(BF16), v_ref[...], preferred_element_type=F32)
    o_ref[...] = (o / denom).astype(o_ref.dtype)


def _attn(q, k, v, *, batch, seq, n_mem, n_heads, name):
    t, d = q.shape
    dh = d // n_heads
    bq = _blk(seq, 1024)
    nq = seq // bq
    return pl.pallas_call(
        functools.partial(_attn_kernel, scale=dh ** -0.5),
        grid=(batch, n_heads, nq),
        in_specs=[
            pl.BlockSpec((bq, dh), lambda b, h, i: (b * nq + i, h)),
            pl.BlockSpec((n_mem, dh), lambda b, h, i: (b, h)),
            pl.BlockSpec((n_mem, dh), lambda b, h, i: (b, h)),
        ],
        out_specs=pl.BlockSpec((bq, dh), lambda b, h, i: (b * nq + i, h)),
        out_shape=jax.ShapeDtypeStruct((t, d), BF16),
        compiler_params=_params(("parallel", "parallel", "arbitrary")),
        name=name,
    )(q, k, v)


def kernel(x, mem, ffn1_norm, ffn1_w_in, ffn1_w_out, mix_norm, mix_w_in, ssm_a_re, ssm_a_im, ssm_log_dt, ssm_b_re, ssm_b_im, ssm_c_re, ssm_c_im, ssm_d, ssm_glu_w, conv_w, conv_w_out, mix_w_out, xattn_norm, mem_norm, xattn_wq, xattn_wk, xattn_wv, xattn_wo, ffn2_norm, ffn2_w_in, ffn2_w_out, final_norm):
    batch, seq, d_model = x.shape
    n_mem = mem.shape[1]
    depth = ffn1_norm.shape[0]
    d_ssm = ssm_d.shape[1]
    d_conv = conv_w.shape[2]
    n_heads = 4
    t = batch * seq
    assert seq % SSM_CHUNK == 0
    n_chunks = seq // SSM_CHUNK

    h = x.reshape(t, d_model)
    memf = mem.reshape(batch * n_mem, d_model)
    for l in range(depth):
        last = l == depth - 1
        h = _ffn(h, ffn1_norm[l], ffn1_w_in[l], ffn1_w_out[l], ffn1_norm[l], final_norm=False, name="ffn1")

        proj = _norm_mm(h, mix_norm[l], mix_w_in[l], name="mix_in")

        u_t = proj[:, :d_ssm].reshape(batch * n_chunks, SSM_CHUNK, d_ssm).transpose(1, 0, 2)
        ys_t = _ssm(u_t, ssm_a_re[l], ssm_a_im[l], ssm_log_dt[l], ssm_b_re[l], ssm_b_im[l],
                    ssm_c_re[l], ssm_c_im[l], ssm_d[l], n_chunks=n_chunks, name="ssm")
        ys = ys_t.transpose(1, 0, 2).reshape(t, d_ssm)

        cz = _conv(proj, conv_w[l], batch=batch, seq=seq, d_conv=d_conv, col0=d_ssm, name="conv")

        merged = _merge(ys, cz, ssm_glu_w[l], conv_w_out[l], proj, d_model=d_model,
                        gate_col0=d_ssm + 3 * d_conv, name="merge")
        h = _mm_res(merged, mix_w_out[l], h, name="mix_out")

        q = _norm_mm(h, xattn_norm[l], xattn_wq[l], name="xattn_q")
        k = _norm_mm(memf, mem_norm[l], xattn_wk[l], name="xattn_k")
        v = _norm_mm(memf, mem_norm[l], xattn_wv[l], name="xattn_v")
        o = _attn(q, k, v, batch=batch, seq=seq, n_mem=n_mem, n_heads=n_heads, name="xattn")
        h = _mm_res(o, xattn_wo[l], h, name="xattn_o")

        if last:
            h = _ffn(h, ffn2_norm[l], ffn2_w_in[l], ffn2_w_out[l], final_norm, final_norm=True, name="ffn2")
        else:
            h = _ffn(h, ffn2_norm[l], ffn2_w_in[l], ffn2_w_out[l], final_norm, final_norm=False, name="ffn2")
    if depth == 0:
        raise NotImplementedError("depth 0")
    return h.reshape(batch, seq, d_model)
```

```python
import functools

import jax
import jax.numpy as jnp
from jax import lax
from jax.experimental import pallas as pl
from jax.experimental.pallas import tpu as pltpu

F32 = jnp.float32
BF16 = jnp.bfloat16
RMS_EPS = 1e-6

V7X_LANES = 128
V7X_MXU_DIM = 256
V7X_VMEM_BYTES = 64 * 1024 * 1024
VMEM_LIMIT = V7X_VMEM_BYTES - 8 * 1024 * 1024

SSM_CHUNK = 16
SSM_GROUPS_PER_BLOCK = 16
SSM_SUB = 4


def _params(dims):
    return pltpu.CompilerParams(dimension_semantics=dims, vmem_limit_bytes=VMEM_LIMIT)


def _blk(dim, pref):
    b = min(dim, pref)
    assert dim % b == 0, (dim, pref)
    return b


def _rms_rows(x, g):
    ms = jnp.mean(x * x, axis=-1, keepdims=True)
    return x * lax.rsqrt(ms + RMS_EPS) * g


def _row_chunks(n_rows, chunk, body):
    chunk = min(chunk, n_rows)
    assert n_rows % chunk == 0

    def step(c, carry):
        body(pl.ds(pl.multiple_of(c * chunk, chunk), chunk))
        return carry

    lax.fori_loop(0, n_rows // chunk, step, 0)


def _ffn_kernel(x_ref, g_ref, wa_ref, wb_ref, wout_ref, g2_ref, o_ref, xn_ref, *, final_norm):
    f = pl.program_id(1)
    bm = x_ref.shape[0]

    @pl.when(f == 0)
    def _():
        def body(rows):
            xn_ref[rows, :] = _rms_rows(x_ref[rows, :], g_ref[...]).astype(BF16)
            o_ref[rows, :] = jnp.zeros((rows.size, o_ref.shape[1]), F32)

        _row_chunks(bm, 64, body)

    xn = xn_ref[...]
    a = jnp.dot(xn, wa_ref[...], preferred_element_type=F32)
    b = jnp.dot(xn, wb_ref[...], preferred_element_type=F32)
    gated = (a * jax.nn.sigmoid(a) * b).astype(BF16)
    d = o_ref.shape[1]
    bn = min(d, 1024)
    for n0 in range(0, d, bn):
        o_ref[:, n0:n0 + bn] += jnp.dot(gated, wout_ref[:, n0:n0 + bn], preferred_element_type=F32)

    @pl.when(f == pl.num_programs(1) - 1)
    def _():
        def body(rows):
            h = x_ref[rows, :] + 0.5 * o_ref[rows, :]
            if final_norm:
                h = _rms_rows(h, g2_ref[...])
            o_ref[rows, :] = h

        _row_chunks(bm, 64, body)


def _ffn(x, g, w_in, w_out, g2, *, final_norm, name):
    t, d = x.shape
    f_dim = w_out.shape[0]
    bf = _blk(f_dim, V7X_MXU_DIM)
    nf = f_dim // bf
    bm = _blk(t, 512)
    w_in = w_in.astype(BF16)
    return pl.pallas_call(
        functools.partial(_ffn_kernel, final_norm=final_norm),
        grid=(t // bm, nf),
        in_specs=[
            pl.BlockSpec((bm, d), lambda i, f: (i, 0)),
            pl.BlockSpec((1, d), lambda i, f: (0, 0)),
            pl.BlockSpec((d, bf), lambda i, f: (0, f)),
            pl.BlockSpec((d, bf), lambda i, f: (0, nf + f)),
            pl.BlockSpec((bf, d), lambda i, f: (f, 0)),
            pl.BlockSpec((1, d), lambda i, f: (0, 0)),
        ],
        out_specs=pl.BlockSpec((bm, d), lambda i, f: (i, 0)),
        out_shape=jax.ShapeDtypeStruct((t, d), F32),
        scratch_shapes=[pltpu.VMEM((bm, d), BF16)],
        compiler_params=_params(("parallel", "arbitrary")),
        name=name,
    )(x, g.reshape(1, d), w_in, w_in, w_out.astype(BF16), g2.reshape(1, d))


def _norm_mm_kernel(a_ref, g_ref, w_ref, o_ref, an_ref):
    @pl.when(pl.program_id(1) == 0)
    def _():
        def body(rows):
            an_ref[rows, :] = _rms_rows(a_ref[rows, :], g_ref[...]).astype(BF16)

        _row_chunks(a_ref.shape[0], 64, body)

    o_ref[...] = jnp.dot(an_ref[...], w_ref[...], preferred_element_type=F32).astype(o_ref.dtype)


def _norm_mm(a, g, w, *, name, bm=512, bn=1024):
    m, k = a.shape
    n = w.shape[1]
    bm = _blk(m, bm)
    bn = _blk(n, bn)
    return pl.pallas_call(
        _norm_mm_kernel,
        grid=(m // bm, n // bn),
        in_specs=[
            pl.BlockSpec((bm, k), lambda i, j: (i, 0)),
            pl.BlockSpec((1, k), lambda i, j: (0, 0)),
            pl.BlockSpec((k, bn), lambda i, j: (0, j)),
        ],
        out_specs=pl.BlockSpec((bm, bn), lambda i, j: (i, j)),
        out_shape=jax.ShapeDtypeStruct((m, n), BF16),
        scratch_shapes=[pltpu.VMEM((bm, k), BF16)],
        compiler_params=_params(("parallel", "arbitrary")),
        name=name,
    )(a, g.reshape(1, k), w.astype(BF16))


def _mm_res_kernel(a_ref, w_ref, res_ref, o_ref):
    o_ref[...] = res_ref[...] + jnp.dot(a_ref[...], w_ref[...], preferred_element_type=F32)


def _mm_res(a, w, res, *, name, bm=1024, bn=1024):
    m, k = a.shape
    n = w.shape[1]
    bm = _blk(m, bm)
    bn = _blk(n, bn)
    return pl.pallas_call(
        _mm_res_kernel,
        grid=(m // bm, n // bn),
        in_specs=[
            pl.BlockSpec((bm, k), lambda i, j: (i, 0)),
            pl.BlockSpec((k, bn), lambda i, j: (0, j)),
            pl.BlockSpec((bm, bn), lambda i, j: (i, j)),
        ],
        out_specs=pl.BlockSpec((bm, bn), lambda i, j: (i, j)),
        out_shape=jax.ShapeDtypeStruct((m, n), F32),
        compiler_params=_params(("parallel", "arbitrary")),
        name=name,
    )(a, w.astype(BF16), res)


def _cmul(ar, ai, br, bi):
    return ar * br - ai * bi, ar * bi + ai * br


def _gelu_exact(x):
    return 0.5 * x * (1.0 + lax.erf(x * (2.0 ** -0.5)))


def _ssm_kernel(tok_ref, are_ref, aim_ref, ldt_ref, braw_ref, craw_ref, dsk_ref, o_ref,
                bstack_ref, nst_ref, dstack_ref, pw_ref, apow_ref, sre_ref, sim_ref, w_ref,
                slab_ref, u_ref, *, n_seq, n_chunks, n_log):
    w256 = V7X_MXU_DIM
    ns = braw_ref.shape[2] // 2
    n_strip = ns // V7X_LANES
    off = sre_ref.shape[2] - n_chunks
    rows = n_seq * n_chunks
    n_slab = w256 // V7X_LANES

    for hf in range(n_slab):
        slab_ref[hf] = tok_ref[:, hf * V7X_LANES:(hf + 1) * V7X_LANES].astype(F32)
    for r in range(SSM_CHUNK):
        parts = [slab_ref[hf, pl.ds(r, rows, stride=SSM_CHUNK), :] for hf in range(n_slab)]
        u_ref[r] = jnp.concatenate(parts, axis=1).astype(BF16)

    @pl.when(pl.program_id(1) == 0)
    def _prepare():
        ar = are_ref[0]
        ai = aim_ref[0]
        dt = jnp.exp(ldt_ref[0])
        mag = jnp.exp(ar * dt)
        lr = mag * jnp.cos(ai * dt)
        li = mag * jnp.sin(ai * dt)
        den = ar * ar + ai * ai
        zr = ((lr - 1.0) * ar + li * ai) / den
        zi = (li * ar - (lr - 1.0) * ai) / den
        sr, si = _cmul(braw_ref[0, :, :ns], braw_ref[0, :, ns:], zr, zi)
        cr = craw_ref[0, :, :ns]
        ci = craw_ref[0, :, ns:]
        cn = jnp.concatenate([cr, -ci], axis=1).astype(BF16)
        for j in range(SSM_CHUNK):
            bs = jnp.concatenate([sr, si], axis=1).astype(BF16)
            dj = lax.dot_general(bs, cn, (((1,), (1,)), ((), ())), preferred_element_type=F32)
            dstack_ref[(SSM_CHUNK - 1 - j) * w256:(SSM_CHUNK - j) * w256, :] = dj.astype(BF16)
            if j < SSM_SUB:
                bstack_ref[(SSM_SUB - 1 - j) * w256:(SSM_SUB - j) * w256, :] = bs
            sr, si = _cmul(sr, si, lr, li)
        mr, mi = lr, li
        for s in range(SSM_SUB):
            er, ei = _cmul(cr, ci, mr, mi)
            nst_ref[s * w256:(s + 1) * w256, :] = jnp.concatenate([er, -ei], axis=1).astype(BF16)
            if s < SSM_SUB - 1:
                mr, mi = _cmul(mr, mi, lr, li)
        pw_ref[0:1, :] = mr
        pw_ref[1:2, :] = mi
        p8r, p8i = _cmul(mr, mi, mr, mi)
        kr, ki = _cmul(p8r, p8i, p8r, p8i)
        for k in range(n_log):
            for s in range(n_strip):
                apow_ref[k, s, 0:1, :] = kr[:, s * V7X_LANES:(s + 1) * V7X_LANES]
                apow_ref[k, s, 1:2, :] = ki[:, s * V7X_LANES:(s + 1) * V7X_LANES]
            kr, ki = _cmul(kr, ki, kr, ki)
        zeros = jnp.zeros((off, V7X_LANES), F32)
        for s in range(n_strip):
            for q in range(n_seq):
                sre_ref[s, q, 0:off, :] = zeros
                sim_ref[s, q, 0:off, :] = zeros

    p4r = pw_ref[0:1, :]
    p4i = pw_ref[1:2, :]

    xr = xi = None
    for q in range(SSM_SUB):
        z = None
        for s in range(SSM_SUB):
            d = jnp.dot(u_ref[SSM_SUB * q + s], bstack_ref[s * w256:(s + 1) * w256, :],
                        preferred_element_type=F32)
            z = d if z is None else z + d
        if xr is None:
            xr, xi = z[:, :ns], z[:, ns:]
        else:
            xr, xi = _cmul(xr, xi, p4r, p4i)
            xr, xi = xr + z[:, :ns], xi + z[:, ns:]

    for s in range(n_strip):
        lanes = slice(s * V7X_LANES, (s + 1) * V7X_LANES)
        for q in range(n_seq):
            sre_ref[s, q, off:, :] = xr[q * n_chunks:(q + 1) * n_chunks, lanes]
            sim_ref[s, q, off:, :] = xi[q * n_chunks:(q + 1) * n_chunks, lanes]
    for s in range(n_strip):
        for q in range(n_seq):
            for k in range(n_log):
                sh = 1 << k
                kr = apow_ref[k, s, 0:1, :]
                ki = apow_ref[k, s, 1:2, :]
                pr = sre_ref[s, q, off - sh:off - sh + n_chunks, :]
                pi = sim_ref[s, q, off - sh:off - sh + n_chunks, :]
                tr, ti = _cmul(pr, pi, kr, ki)
                sre_ref[s, q, off:, :] = sre_ref[s, q, off:, :] + tr
                sim_ref[s, q, off:, :] = sim_ref[s, q, off:, :] + ti

    ysts = []
    for q in range(SSM_SUB):
        for s in range(n_strip):
            lanes_r = slice(s * V7X_LANES, (s + 1) * V7X_LANES)
            lanes_i = slice(ns + s * V7X_LANES, ns + (s + 1) * V7X_LANES)
            for sq in range(n_seq):
                rs = slice(sq * n_chunks, (sq + 1) * n_chunks)
                if q == 0:
                    vr = sre_ref[s, sq, off - 1:off - 1 + n_chunks, :]
                    vi = sim_ref[s, sq, off - 1:off - 1 + n_chunks, :]
                else:
                    vr, vi = _cmul(sre_ref[s, sq, off:, :], sim_ref[s, sq, off:, :],
                                   p4r[:, lanes_r], p4i[:, lanes_r])
                if q < SSM_SUB - 1:
                    sre_ref[s, sq, off:, :] = vr
                    sim_ref[s, sq, off:, :] = vi
                w_ref[rs, lanes_r] = vr.astype(BF16)
                w_ref[rs, lanes_i] = vi.astype(BF16)
        ysts.append(lax.dot_general(w_ref[...], nst_ref[...], (((1,), (1,)), ((), ())),
                                    preferred_element_type=F32))

    dsk = dsk_ref[0]
    for t in range(SSM_CHUNK):
        y = ysts[t // SSM_SUB][:, (t % SSM_SUB) * w256:(t % SSM_SUB + 1) * w256]
        y = y + dsk * u_ref[t].astype(F32)
        for r in range(t + 1):
            blk = SSM_CHUNK - 1 - t + r
            y = y + jnp.dot(u_ref[r], dstack_ref[blk * w256:(blk + 1) * w256, :],
                            preferred_element_type=F32)
        y = _gelu_exact(y)
        for hf in range(n_slab):
            slab_ref[hf, pl.ds(t, rows, stride=SSM_CHUNK), :] = y[:, hf * V7X_LANES:(hf + 1) * V7X_LANES]
    for hf in range(n_slab):
        o_ref[:, hf * V7X_LANES:(hf + 1) * V7X_LANES] = slab_ref[hf].astype(o_ref.dtype)


def _ssm(proj, a_re, a_im, log_dt, b_re, b_im, c_re, c_im, d_skip, *, n_chunks, name):
    n_tok = proj.shape[0]
    d_ssm = d_skip.shape[0]
    n_rows = n_tok // SSM_CHUNK
    n_groups, n_state = a_re.shape
    h = d_ssm // n_groups
    gpb = SSM_GROUPS_PER_BLOCK
    assert gpb * h == V7X_MXU_DIM and n_groups % gpb == 0
    n_blocks = n_groups // gpb
    ns = gpb * n_state
    n_seq = 1
    n_log = max(1, (n_chunks - 1).bit_length())
    off = max(8, 1 << (n_log - 1))
    eye = jnp.eye(gpb, dtype=F32)

    def block_diag(x):
        x = x.reshape(n_blocks, gpb, h, n_state).transpose(0, 2, 1, 3)
        x = eye[None, :, None, :, None] * x[:, None, :, :, :]
        return x.reshape(n_blocks, gpb * h, ns)

    braw = jnp.concatenate([block_diag(b_re.transpose(0, 2, 1)), block_diag(b_im.transpose(0, 2, 1))], axis=2)
    craw = jnp.concatenate([block_diag(c_re), block_diag(c_im)], axis=2)
    vec = lambda x: x.reshape(n_blocks, 1, ns)
    ldt = jnp.broadcast_to(log_dt[:, None], (n_groups, n_state))
    rows = n_seq * n_chunks
    kernel = functools.partial(_ssm_kernel, n_seq=n_seq, n_chunks=n_chunks, n_log=n_log)
    vspec = pl.BlockSpec((1, 1, ns), lambda g, r: (g, 0, 0))
    mspec = pl.BlockSpec((1, V7X_MXU_DIM, 2 * ns), lambda g, r: (g, 0, 0))
    uspec = pl.BlockSpec((rows * SSM_CHUNK, V7X_MXU_DIM), lambda g, r: (r, g))
    return pl.pallas_call(
        kernel,
        grid=(n_blocks, n_rows // rows),
        in_specs=[uspec, vspec, vspec, vspec, mspec, mspec,
                  pl.BlockSpec((1, 1, V7X_MXU_DIM), lambda g, r: (g, 0, 0))],
        out_specs=uspec,
        out_shape=jax.ShapeDtypeStruct((n_tok, d_ssm), BF16),
        scratch_shapes=[
            pltpu.VMEM((SSM_SUB * V7X_MXU_DIM, 2 * ns), BF16),
            pltpu.VMEM((SSM_SUB * V7X_MXU_DIM, 2 * ns), BF16),
            pltpu.VMEM((SSM_CHUNK * V7X_MXU_DIM, V7X_MXU_DIM), BF16),
            pltpu.VMEM((8, ns), F32),
            pltpu.VMEM((n_log, ns // V7X_LANES, 8, V7X_LANES), F32),
            pltpu.VMEM((ns // V7X_LANES, n_seq, off + n_chunks, V7X_LANES), F32),
            pltpu.VMEM((ns // V7X_LANES, n_seq, off + n_chunks, V7X_LANES), F32),
            pltpu.VMEM((rows, 2 * ns), BF16),
            pltpu.VMEM((V7X_MXU_DIM // V7X_LANES, rows * SSM_CHUNK, V7X_LANES), F32),
            pltpu.VMEM((SSM_CHUNK, rows, V7X_MXU_DIM), BF16),
        ],
        compiler_params=_params(("parallel", "arbitrary")),
        name=name,
    )(proj, vec(a_re), vec(a_im), vec(ldt), braw, craw, d_skip.reshape(n_blocks, 1, V7X_MXU_DIM))


def _conv_kernel(cb_ref, cc_ref, ch_ref, w_ref, o_ref, z_ref, *, chunk):
    seq = cb_ref.shape[0]
    pad = z_ref.shape[0] - seq
    kw = w_ref.shape[0]
    z_ref[0:pad, :] = jnp.zeros((pad, z_ref.shape[1]), F32)

    def fill(rows):
        z_ref[pl.ds(rows.start + pad, rows.size), :] = cc_ref[rows, :].astype(F32) * ch_ref[rows, :].astype(F32)

    _row_chunks(seq, chunk, fill)
    for c in range(seq // chunk):
        r0 = c * chunk
        acc = None
        for k in range(kw):
            lag = kw - 1 - k
            term = w_ref[k:k + 1, :] * z_ref[pad + r0 - lag:pad + r0 - lag + chunk, :]
            acc = term if acc is None else acc + term
        o_ref[r0:r0 + chunk, :] = (cb_ref[r0:r0 + chunk, :].astype(F32) * acc).astype(o_ref.dtype)


def _conv(proj, conv_w, *, batch, seq, d_conv, col0, name):
    bc = _blk(d_conv, V7X_MXU_DIM)
    nb = d_conv // bc
    chunk = min(seq, 512)
    kw = conv_w.shape[0]
    cspec = lambda k: pl.BlockSpec((seq, bc), lambda b, j: (b, (col0 + k * d_conv) // bc + j))
    return pl.pallas_call(
        functools.partial(_conv_kernel, chunk=chunk),
        grid=(batch, nb),
        in_specs=[cspec(0), cspec(1), cspec(2), pl.BlockSpec((kw, bc), lambda b, j: (0, j))],
        out_specs=pl.BlockSpec((seq, bc), lambda b, j: (b, j)),
        out_shape=jax.ShapeDtypeStruct((batch * seq, d_conv), BF16),
        scratch_shapes=[pltpu.VMEM((8 + seq, bc), F32)],
        compiler_params=_params(("parallel", "parallel")),
        name=name,
    )(proj, proj, proj, conv_w)


def _merge_kernel(ys_ref, cz_ref, wv_ref, wg_ref, wc_ref, ga_ref, gb_ref, o_ref):
    ys = ys_ref[...]
    val = jnp.dot(ys, wv_ref[...], preferred_element_type=F32)
    gate = jnp.dot(ys, wg_ref[...], preferred_element_type=F32)
    yb = jnp.dot(cz_ref[...], wc_ref[...], preferred_element_type=F32)
    ya = val * jax.nn.sigmoid(gate)
    out = jax.nn.sigmoid(ga_ref[...].astype(F32)) * ya + jax.nn.sigmoid(gb_ref[...].astype(F32)) * yb
    o_ref[...] = out.astype(o_ref.dtype)


def _merge(ys, cz, glu_w, conv_w_out, proj, *, d_model, gate_col0, name, bm=1024, bn=512):
    t, k = ys.shape
    bm = _blk(t, bm)
    bn = _blk(d_model, bn)
    nj = d_model // bn
    aspec = pl.BlockSpec((bm, k), lambda i, j: (i, 0))
    wspec = lambda off: pl.BlockSpec((k, bn), lambda i, j: (0, off + j))
    gspec = lambda off: pl.BlockSpec((bm, bn), lambda i, j: (i, off + j))
    glu_w = glu_w.astype(BF16)
    return pl.pallas_call(
        _merge_kernel,
        grid=(t // bm, nj),
        in_specs=[aspec, aspec, wspec(0), wspec(nj), wspec(0),
                  gspec(gate_col0 // bn), gspec(gate_col0 // bn + nj)],
        out_specs=pl.BlockSpec((bm, bn), lambda i, j: (i, j)),
        out_shape=jax.ShapeDtypeStruct((t, d_model), BF16),
        compiler_params=_params(("parallel", "arbitrary")),
        name=name,
    )(ys, cz, glu_w, glu_w, conv_w_out.astype(BF16), proj, proj)


def _attn_kernel(q_ref, k_ref, v_ref, o_ref, *, scale):
    s = lax.dot_general(q_ref[...], k_ref[...], (((1,), (1,)), ((), ())), preferred_element_type=F32) * scale
    p = jnp.exp(s - jnp.max(s, axis=-1, keepdims=True))
    denom = jnp.sum(p, axis=-1, keepdims=True)
    o = jnp.dot(p.astype(BF16), v_ref[...], preferred_element_type=F32)
    o_ref[...] = (o / denom).astype(o_ref.dtype)


def _attn(q, k, v, *, batch, seq, n_mem, n_heads, name):
    t, d = q.shape
    dh = d // n_heads
    bq = _blk(seq, 1024)
    nq = seq // bq
    return pl.pallas_call(
        functools.partial(_attn_kernel, scale=dh ** -0.5),
        grid=(batch, n_heads, nq),
        in_specs=[
            pl.BlockSpec((bq, dh), lambda b, h, i: (b * nq + i, h)),
            pl.BlockSpec((n_mem, dh), lambda b, h, i: (b, h)),
            pl.BlockSpec((n_mem, dh), lambda b, h, i: (b, h)),
        ],
        out_specs=pl.BlockSpec((bq, dh), lambda b, h, i: (b * nq + i, h)),
        out_shape=jax.ShapeDtypeStruct((t, d), BF16),
        compiler_params=_params(("parallel", "parallel", "arbitrary")),
        name=name,
    )(q, k, v)


def kernel(x, mem, ffn1_norm, ffn1_w_in, ffn1_w_out, mix_norm, mix_w_in, ssm_a_re, ssm_a_im, ssm_log_dt, ssm_b_re, ssm_b_im, ssm_c_re, ssm_c_im, ssm_d, ssm_glu_w, conv_w, conv_w_out, mix_w_out, xattn_norm, mem_norm, xattn_wq, xattn_wk, xattn_wv, xattn_wo, ffn2_norm, ffn2_w_in, ffn2_w_out, final_norm):
    batch, seq, d_model = x.shape
    n_mem = mem.shape[1]
    depth = ffn1_norm.shape[0]
    d_ssm = ssm_d.shape[1]
    d_conv = conv_w.shape[2]
    n_heads = 4
    t = batch * seq
    assert seq % SSM_CHUNK == 0
    n_chunks = seq // SSM_CHUNK

    h = x.reshape(t, d_model)
    memf = mem.reshape(batch * n_mem, d_model)
    for l in range(depth):
        last = l == depth - 1
        h = _ffn(h, ffn1_norm[l], ffn1_w_in[l], ffn1_w_out[l], ffn1_norm[l], final_norm=False, name="ffn1")

        proj = _norm_mm(h, mix_norm[l], mix_w_in[l], name="mix_in")

        ys = _ssm(proj, ssm_a_re[l], ssm_a_im[l], ssm_log_dt[l], ssm_b_re[l], ssm_b_im[l],
                  ssm_c_re[l], ssm_c_im[l], ssm_d[l], n_chunks=n_chunks, name="ssm")

        cz = _conv(proj, conv_w[l], batch=batch, seq=seq, d_conv=d_conv, col0=d_ssm, name="conv")

        merged = _merge(ys, cz, ssm_glu_w[l], conv_w_out[l], proj, d_model=d_model,
                        gate_col0=d_ssm + 3 * d_conv, name="merge")
        h = _mm_res(merged, mix_w_out[l], h, name="mix_out")

        q = _norm_mm(h, xattn_norm[l], xattn_wq[l], name="xattn_q")
        k = _norm_mm(memf, mem_norm[l], xattn_wk[l], name="xattn_k")
        v = _norm_mm(memf, mem_norm[l], xattn_wv[l], name="xattn_v")
        o = _attn(q, k, v, batch=batch, seq=seq, n_mem=n_mem, n_heads=n_heads, name="xattn")
        h = _mm_res(o, xattn_wo[l], h, name="xattn_o")

        if last:
            h = _ffn(h, ffn2_norm[l], ffn2_w_in[l], ffn2_w_out[l], final_norm, final_norm=True, name="ffn2")
        else:
            h = _ffn(h, ffn2_norm[l], ffn2_w_in[l], ffn2_w_out[l], final_norm, final_norm=False, name="ffn2")
    if depth == 0:
        raise NotImplementedError("depth 0")
    return h.reshape(batch, seq, d_model)
```

```python
import functools

import jax
import jax.numpy as jnp
from jax import lax
from jax.experimental import pallas as pl
from jax.experimental.pallas import tpu as pltpu

F32 = jnp.float32
BF16 = jnp.bfloat16
RMS_EPS = 1e-6

V7X_LANES = 128
V7X_MXU_DIM = 256
V7X_VMEM_BYTES = 64 * 1024 * 1024
VMEM_LIMIT = V7X_VMEM_BYTES - 8 * 1024 * 1024

SSM_CHUNK = 16
SSM_GROUPS_PER_BLOCK = 16
SSM_SUB = 4


def _params(dims):
    return pltpu.CompilerParams(dimension_semantics=dims, vmem_limit_bytes=VMEM_LIMIT)


def _blk(dim, pref):
    b = min(dim, pref)
    assert dim % b == 0, (dim, pref)
    return b


def _rms_rows(x, g):
    ms = jnp.mean(x * x, axis=-1, keepdims=True)
    return x * lax.rsqrt(ms + RMS_EPS) * g


def _row_chunks(n_rows, chunk, body):
    chunk = min(chunk, n_rows)
    assert n_rows % chunk == 0

    def step(c, carry):
        body(pl.ds(pl.multiple_of(c * chunk, chunk), chunk))
        return carry

    lax.fori_loop(0, n_rows // chunk, step, 0)


BF16_SUBLANES = 16


def _side_spec(shape, grid):
    r, c = shape
    n_steps = grid[0] * grid[1]
    best = None
    for rb in range(BF16_SUBLANES, r + 1, BF16_SUBLANES):
        if r % rb:
            continue
        for cb in range(V7X_LANES, c + 1, V7X_LANES):
            if c % cb == 0 and (r // rb) * (c // cb) <= n_steps and (best is None or rb * cb < best[0] * best[1]):
                best = (rb, cb)
    assert best is not None, (shape, grid)
    rb, cb = best
    ncb = c // cb
    last = (r // rb) * ncb - 1

    def index(i, j):
        blk = jnp.minimum(i * grid[1] + j, last)
        return blk // ncb, blk % ncb

    return pl.BlockSpec((rb, cb), index)


def _call(body, *, grid, in_specs, out_spec, out_shape, scratch_shapes=(), args, side=(), name):
    n_in, n_side = len(in_specs), len(side)

    def kernel(*refs):
        side_in = refs[n_in:n_in + n_side]
        side_out = refs[n_in + n_side + 1:n_in + 2 * n_side + 1]
        for src, dst in zip(side_in, side_out):
            dst[...] = src[...].astype(BF16)
        body(*refs[:n_in], refs[n_in + n_side], *refs[n_in + 2 * n_side + 1:])

    side_specs = [_side_spec(w.shape, grid) for w in side]
    outs = pl.pallas_call(
        kernel,
        grid=grid,
        in_specs=list(in_specs) + side_specs,
        out_specs=[out_spec] + side_specs,
        out_shape=[out_shape] + [jax.ShapeDtypeStruct(w.shape, BF16) for w in side],
        scratch_shapes=list(scratch_shapes),
        compiler_params=_params(("arbitrary",) * len(grid)),
        name=name,
    )(*args, *side)
    return outs[0], tuple(outs[1:])


def _ffn_kernel(x_ref, g_ref, wa_ref, wb_ref, wout_ref, g2_ref, o_ref, xn_ref, *, final_norm):
    f = pl.program_id(1)
    bm = x_ref.shape[0]

    @pl.when(f == 0)
    def _():
        def body(rows):
            xn_ref[rows, :] = _rms_rows(x_ref[rows, :], g_ref[...]).astype(BF16)
            o_ref[rows, :] = jnp.zeros((rows.size, o_ref.shape[1]), F32)

        _row_chunks(bm, 64, body)

    xn = xn_ref[...]
    a = jnp.dot(xn, wa_ref[...], preferred_element_type=F32)
    b = jnp.dot(xn, wb_ref[...], preferred_element_type=F32)
    gated = (a * jax.nn.sigmoid(a) * b).astype(BF16)
    d = o_ref.shape[1]
    bn = min(d, 1024)
    for n0 in range(0, d, bn):
        o_ref[:, n0:n0 + bn] += jnp.dot(gated, wout_ref[:, n0:n0 + bn], preferred_element_type=F32)

    @pl.when(f == pl.num_programs(1) - 1)
    def _():
        def body(rows):
            h = x_ref[rows, :] + 0.5 * o_ref[rows, :]
            if final_norm:
                h = _rms_rows(h, g2_ref[...])
            o_ref[rows, :] = h

        _row_chunks(bm, 64, body)


def _ffn(x, g, w_in, w_out, g2, *, final_norm, name, side=()):
    t, d = x.shape
    f_dim = w_out.shape[0]
    bf = _blk(f_dim, V7X_MXU_DIM)
    nf = f_dim // bf
    bm = _blk(t, 512)
    w_in = w_in.astype(BF16)
    return _call(
        functools.partial(_ffn_kernel, final_norm=final_norm),
        grid=(t // bm, nf),
        in_specs=[
            pl.BlockSpec((bm, d), lambda i, f: (i, 0)),
            pl.BlockSpec((1, d), lambda i, f: (0, 0)),
            pl.BlockSpec((d, bf), lambda i, f: (0, f)),
            pl.BlockSpec((d, bf), lambda i, f: (0, nf + f)),
            pl.BlockSpec((bf, d), lambda i, f: (f, 0)),
            pl.BlockSpec((1, d), lambda i, f: (0, 0)),
        ],
        out_spec=pl.BlockSpec((bm, d), lambda i, f: (i, 0)),
        out_shape=jax.ShapeDtypeStruct((t, d), F32),
        scratch_shapes=[pltpu.VMEM((bm, d), BF16)],
        args=(x, g.reshape(1, d), w_in, w_in, w_out.astype(BF16), g2.reshape(1, d)),
        side=side,
        name=name,
    )


def _norm_mm_kernel(a_ref, g_ref, w_ref, o_ref, an_ref):
    @pl.when(pl.program_id(1) == 0)
    def _():
        def body(rows):
            an_ref[rows, :] = _rms_rows(a_ref[rows, :], g_ref[...]).astype(BF16)

        _row_chunks(a_ref.shape[0], 64, body)

    o_ref[...] = jnp.dot(an_ref[...], w_ref[...], preferred_element_type=F32).astype(o_ref.dtype)


def _norm_mm(a, g, w, *, name, bm=512, bn=1024, side=()):
    m, k = a.shape
    n = w.shape[1]
    bm = _blk(m, bm)
    bn = _blk(n, bn)
    return _call(
        _norm_mm_kernel,
        grid=(m // bm, n // bn),
        in_specs=[
            pl.BlockSpec((bm, k), lambda i, j: (i, 0)),
            pl.BlockSpec((1, k), lambda i, j: (0, 0)),
            pl.BlockSpec((k, bn), lambda i, j: (0, j)),
        ],
        out_spec=pl.BlockSpec((bm, bn), lambda i, j: (i, j)),
        out_shape=jax.ShapeDtypeStruct((m, n), BF16),
        scratch_shapes=[pltpu.VMEM((bm, k), BF16)],
        args=(a, g.reshape(1, k), w.astype(BF16)),
        side=side,
        name=name,
    )


def _mm_res_kernel(a_ref, w_ref, res_ref, o_ref):
    o_ref[...] = res_ref[...] + jnp.dot(a_ref[...], w_ref[...], preferred_element_type=F32)


def _mm_res(a, w, res, *, name, bm=1024, bn=1024):
    m, k = a.shape
    n = w.shape[1]
    bm = _blk(m, bm)
    bn = _blk(n, bn)
    return pl.pallas_call(
        _mm_res_kernel,
        grid=(m // bm, n // bn),
        in_specs=[
            pl.BlockSpec((bm, k), lambda i, j: (i, 0)),
            pl.BlockSpec((k, bn), lambda i, j: (0, j)),
            pl.BlockSpec((bm, bn), lambda i, j: (i, j)),
        ],
        out_specs=pl.BlockSpec((bm, bn), lambda i, j: (i, j)),
        out_shape=jax.ShapeDtypeStruct((m, n), F32),
        compiler_params=_params(("parallel", "arbitrary")),
        name=name,
    )(a, w.astype(BF16), res)


def _cmul(ar, ai, br, bi):
    return ar * br - ai * bi, ar * bi + ai * br


def _gelu_exact(x):
    return 0.5 * x * (1.0 + lax.erf(x * (2.0 ** -0.5)))


def _ssm_kernel(tok_ref, are_ref, aim_ref, ldt_ref, braw_ref, craw_ref, dsk_ref, o_ref,
                bstack_ref, nst_ref, dstack_ref, pw_ref, apow_ref, sre_ref, sim_ref, w_ref,
                slab_ref, u_ref, *, n_seq, n_chunks, n_log):
    w256 = V7X_MXU_DIM
    ns = braw_ref.shape[2] // 2
    n_strip = ns // V7X_LANES
    off = sre_ref.shape[2] - n_chunks
    rows = n_seq * n_chunks
    n_slab = w256 // V7X_LANES

    for hf in range(n_slab):
        slab_ref[hf] = tok_ref[:, hf * V7X_LANES:(hf + 1) * V7X_LANES].astype(F32)
    for r in range(SSM_CHUNK):
        parts = [slab_ref[hf, pl.ds(r, rows, stride=SSM_CHUNK), :] for hf in range(n_slab)]
        u_ref[r] = jnp.concatenate(parts, axis=1).astype(BF16)

    @pl.when(pl.program_id(1) == 0)
    def _prepare():
        ar = are_ref[0]
        ai = aim_ref[0]
        dt = jnp.exp(ldt_ref[0])
        mag = jnp.exp(ar * dt)
        lr = mag * jnp.cos(ai * dt)
        li = mag * jnp.sin(ai * dt)
        den = ar * ar + ai * ai
        zr = ((lr - 1.0) * ar + li * ai) / den
        zi = (li * ar - (lr - 1.0) * ai) / den
        sr, si = _cmul(braw_ref[0, :, :ns], braw_ref[0, :, ns:], zr, zi)
        cr = craw_ref[0, :, :ns]
        ci = craw_ref[0, :, ns:]
        cn = jnp.concatenate([cr, -ci], axis=1).astype(BF16)
        for j in range(SSM_CHUNK):
            bs = jnp.concatenate([sr, si], axis=1).astype(BF16)
            dj = lax.dot_general(bs, cn, (((1,), (1,)), ((), ())), preferred_element_type=F32)
            dstack_ref[(SSM_CHUNK - 1 - j) * w256:(SSM_CHUNK - j) * w256, :] = dj.astype(BF16)
            if j < SSM_SUB:
                bstack_ref[(SSM_SUB - 1 - j) * w256:(SSM_SUB - j) * w256, :] = bs
            sr, si = _cmul(sr, si, lr, li)
        mr, mi = lr, li
        for s in range(SSM_SUB):
            er, ei = _cmul(cr, ci, mr, mi)
            nst_ref[s * w256:(s + 1) * w256, :] = jnp.concatenate([er, -ei], axis=1).astype(BF16)
            if s < SSM_SUB - 1:
                mr, mi = _cmul(mr, mi, lr, li)
        pw_ref[0:1, :] = mr
        pw_ref[1:2, :] = mi
        p8r, p8i = _cmul(mr, mi, mr, mi)
        kr, ki = _cmul(p8r, p8i, p8r, p8i)
        for k in range(n_log):
            for s in range(n_strip):
                apow_ref[k, s, 0:1, :] = kr[:, s * V7X_LANES:(s + 1) * V7X_LANES]
                apow_ref[k, s, 1:2, :] = ki[:, s * V7X_LANES:(s + 1) * V7X_LANES]
            kr, ki = _cmul(kr, ki, kr, ki)
        zeros = jnp.zeros((off, V7X_LANES), F32)
        for s in range(n_strip):
            for q in range(n_seq):
                sre_ref[s, q, 0:off, :] = zeros
                sim_ref[s, q, 0:off, :] = zeros

    p4r = pw_ref[0:1, :]
    p4i = pw_ref[1:2, :]

    xr = xi = None
    for q in range(SSM_SUB):
        z = None
        for s in range(SSM_SUB):
            d = jnp.dot(u_ref[SSM_SUB * q + s], bstack_ref[s * w256:(s + 1) * w256, :],
                        preferred_element_type=F32)
            z = d if z is None else z + d
        if xr is None:
            xr, xi = z[:, :ns], z[:, ns:]
        else:
            xr, xi = _cmul(xr, xi, p4r, p4i)
            xr, xi = xr + z[:, :ns], xi + z[:, ns:]

    for s in range(n_strip):
        lanes = slice(s * V7X_LANES, (s + 1) * V7X_LANES)
        for q in range(n_seq):
            sre_ref[s, q, off:, :] = xr[q * n_chunks:(q + 1) * n_chunks, lanes]
            sim_ref[s, q, off:, :] = xi[q * n_chunks:(q + 1) * n_chunks, lanes]
    for s in range(n_strip):
        for q in range(n_seq):
            for k in range(n_log):
                sh = 1 << k
                kr = apow_ref[k, s, 0:1, :]
                ki = apow_ref[k, s, 1:2, :]
                pr = sre_ref[s, q, off - sh:off - sh + n_chunks, :]
                pi = sim_ref[s, q, off - sh:off - sh + n_chunks, :]
                tr, ti = _cmul(pr, pi, kr, ki)
                sre_ref[s, q, off:, :] = sre_ref[s, q, off:, :] + tr
                sim_ref[s, q, off:, :] = sim_ref[s, q, off:, :] + ti

    ysts = []
    for q in range(SSM_SUB):
        for s in range(n_strip):
            lanes_r = slice(s * V7X_LANES, (s + 1) * V7X_LANES)
            lanes_i = slice(ns + s * V7X_LANES, ns + (s + 1) * V7X_LANES)
            for sq in range(n_seq):
                rs = slice(sq * n_chunks, (sq + 1) * n_chunks)
                if q == 0:
                    vr = sre_ref[s, sq, off - 1:off - 1 + n_chunks, :]
                    vi = sim_ref[s, sq, off - 1:off - 1 + n_chunks, :]
                else:
                    vr, vi = _cmul(sre_ref[s, sq, off:, :], sim_ref[s, sq, off:, :],
                                   p4r[:, lanes_r], p4i[:, lanes_r])
                if q < SSM_SUB - 1:
                    sre_ref[s, sq, off:, :] = vr
                    sim_ref[s, sq, off:, :] = vi
                w_ref[rs, lanes_r] = vr.astype(BF16)
                w_ref[rs, lanes_i] = vi.astype(BF16)
        ysts.append(lax.dot_general(w_ref[...], nst_ref[...], (((1,), (1,)), ((), ())),
                                    preferred_element_type=F32))

    dsk = dsk_ref[0]
    for t in range(SSM_CHUNK):
        y = ysts[t // SSM_SUB][:, (t % SSM_SUB) * w256:(t % SSM_SUB + 1) * w256]
        y = y + dsk * u_ref[t].astype(F32)
        for r in range(t + 1):
            blk = SSM_CHUNK - 1 - t + r
            y = y + jnp.dot(u_ref[r], dstack_ref[blk * w256:(blk + 1) * w256, :],
                            preferred_element_type=F32)
        y = _gelu_exact(y)
        for hf in range(n_slab):
            slab_ref[hf, pl.ds(t, rows, stride=SSM_CHUNK), :] = y[:, hf * V7X_LANES:(hf + 1) * V7X_LANES]
    for hf in range(n_slab):
        o_ref[:, hf * V7X_LANES:(hf + 1) * V7X_LANES] = slab_ref[hf].astype(o_ref.dtype)


def _ssm(proj, a_re, a_im, log_dt, b_re, b_im, c_re, c_im, d_skip, *, n_chunks, name):
    n_tok = proj.shape[0]
    d_ssm = d_skip.shape[0]
    n_rows = n_tok // SSM_CHUNK
    n_groups, n_state = a_re.shape
    h = d_ssm // n_groups
    gpb = SSM_GROUPS_PER_BLOCK
    assert gpb * h == V7X_MXU_DIM and n_groups % gpb == 0
    n_blocks = n_groups // gpb
    ns = gpb * n_state
    n_seq = 1
    n_log = max(1, (n_chunks - 1).bit_length())
    off = max(8, 1 << (n_log - 1))
    eye = jnp.eye(gpb, dtype=F32)

    def block_diag(x):
        x = x.reshape(n_blocks, gpb, h, n_state).transpose(0, 2, 1, 3)
        x = eye[None, :, None, :, None] * x[:, None, :, :, :]
        return x.reshape(n_blocks, gpb * h, ns)

    braw = jnp.concatenate([block_diag(b_re.transpose(0, 2, 1)), block_diag(b_im.transpose(0, 2, 1))], axis=2)
    craw = jnp.concatenate([block_diag(c_re), block_diag(c_im)], axis=2)
    vec = lambda x: x.reshape(n_blocks, 1, ns)
    ldt = jnp.broadcast_to(log_dt[:, None], (n_groups, n_state))
    rows = n_seq * n_chunks
    kernel = functools.partial(_ssm_kernel, n_seq=n_seq, n_chunks=n_chunks, n_log=n_log)
    vspec = pl.BlockSpec((1, 1, ns), lambda g, r: (g, 0, 0))
    mspec = pl.BlockSpec((1, V7X_MXU_DIM, 2 * ns), lambda g, r: (g, 0, 0))
    uspec = pl.BlockSpec((rows * SSM_CHUNK, V7X_MXU_DIM), lambda g, r: (r, g))
    return pl.pallas_call(
        kernel,
        grid=(n_blocks, n_rows // rows),
        in_specs=[uspec, vspec, vspec, vspec, mspec, mspec,
                  pl.BlockSpec((1, 1, V7X_MXU_DIM), lambda g, r: (g, 0, 0))],
        out_specs=uspec,
        out_shape=jax.ShapeDtypeStruct((n_tok, d_ssm), BF16),
        scratch_shapes=[
            pltpu.VMEM((SSM_SUB * V7X_MXU_DIM, 2 * ns), BF16),
            pltpu.VMEM((SSM_SUB * V7X_MXU_DIM, 2 * ns), BF16),
            pltpu.VMEM((SSM_CHUNK * V7X_MXU_DIM, V7X_MXU_DIM), BF16),
            pltpu.VMEM((8, ns), F32),
            pltpu.VMEM((n_log, ns // V7X_LANES, 8, V7X_LANES), F32),
            pltpu.VMEM((ns // V7X_LANES, n_seq, off + n_chunks, V7X_LANES), F32),
            pltpu.VMEM((ns // V7X_LANES, n_seq, off + n_chunks, V7X_LANES), F32),
            pltpu.VMEM((rows, 2 * ns), BF16),
            pltpu.VMEM((V7X_MXU_DIM // V7X_LANES, rows * SSM_CHUNK, V7X_LANES), F32),
            pltpu.VMEM((SSM_CHUNK, rows, V7X_MXU_DIM), BF16),
        ],
        compiler_params=_params(("parallel", "arbitrary")),
        name=name,
    )(proj, vec(a_re), vec(a_im), vec(ldt), braw, craw, d_skip.reshape(n_blocks, 1, V7X_MXU_DIM))


def _conv_kernel(cb_ref, cc_ref, ch_ref, w_ref, o_ref, z_ref, *, chunk):
    seq = cb_ref.shape[0]
    pad = z_ref.shape[0] - seq
    kw = w_ref.shape[0]
    z_ref[0:pad, :] = jnp.zeros((pad, z_ref.shape[1]), F32)

    def fill(rows):
        z_ref[pl.ds(rows.start + pad, rows.size), :] = cc_ref[rows, :].astype(F32) * ch_ref[rows, :].astype(F32)

    _row_chunks(seq, chunk, fill)
    for c in range(seq // chunk):
        r0 = c * chunk
        acc = None
        for k in range(kw):
            lag = kw - 1 - k
            term = w_ref[k:k + 1, :] * z_ref[pad + r0 - lag:pad + r0 - lag + chunk, :]
            acc = term if acc is None else acc + term
        o_ref[r0:r0 + chunk, :] = (cb_ref[r0:r0 + chunk, :].astype(F32) * acc).astype(o_ref.dtype)


def _conv(proj, conv_w, *, batch, seq, d_conv, col0, name):
    bc = _blk(d_conv, V7X_MXU_DIM)
    nb = d_conv // bc
    chunk = min(seq, 512)
    kw = conv_w.shape[0]
    cspec = lambda k: pl.BlockSpec((seq, bc), lambda b, j: (b, (col0 + k * d_conv) // bc + j))
    return pl.pallas_call(
        functools.partial(_conv_kernel, chunk=chunk),
        grid=(batch, nb),
        in_specs=[cspec(0), cspec(1), cspec(2), pl.BlockSpec((kw, bc), lambda b, j: (0, j))],
        out_specs=pl.BlockSpec((seq, bc), lambda b, j: (b, j)),
        out_shape=jax.ShapeDtypeStruct((batch * seq, d_conv), BF16),
        scratch_shapes=[pltpu.VMEM((8 + seq, bc), F32)],
        compiler_params=_params(("parallel", "parallel")),
        name=name,
    )(proj, proj, proj, conv_w)


def _merge_kernel(ys_ref, cz_ref, wv_ref, wg_ref, wc_ref, ga_ref, gb_ref, o_ref):
    ys = ys_ref[...]
    val = jnp.dot(ys, wv_ref[...], preferred_element_type=F32)
    gate = jnp.dot(ys, wg_ref[...], preferred_element_type=F32)
    yb = jnp.dot(cz_ref[...], wc_ref[...], preferred_element_type=F32)
    ya = val * jax.nn.sigmoid(gate)
    out = jax.nn.sigmoid(ga_ref[...].astype(F32)) * ya + jax.nn.sigmoid(gb_ref[...].astype(F32)) * yb
    o_ref[...] = out.astype(o_ref.dtype)


def _merge(ys, cz, glu_w, conv_w_out, proj, *, d_model, gate_col0, name, bm=1024, bn=512, side=()):
    t, k = ys.shape
    bm = _blk(t, bm)
    bn = _blk(d_model, bn)
    nj = d_model // bn
    aspec = pl.BlockSpec((bm, k), lambda i, j: (i, 0))
    wspec = lambda off: pl.BlockSpec((k, bn), lambda i, j: (0, off + j))
    gspec = lambda off: pl.BlockSpec((bm, bn), lambda i, j: (i, off + j))
    glu_w = glu_w.astype(BF16)
    return _call(
        _merge_kernel,
        grid=(t // bm, nj),
        in_specs=[aspec, aspec, wspec(0), wspec(nj), wspec(0),
                  gspec(gate_col0 // bn), gspec(gate_col0 // bn + nj)],
        out_spec=pl.BlockSpec((bm, bn), lambda i, j: (i, j)),
        out_shape=jax.ShapeDtypeStruct((t, d_model), BF16),
        args=(ys, cz, glu_w, glu_w, conv_w_out.astype(BF16), proj, proj),
        side=side,
        name=name,
    )


def _attn_kernel(q_ref, k_ref, v_ref, o_ref, *, scale):
    s = lax.dot_general(q_ref[...], k_ref[...], (((1,), (1,)), ((), ())), preferred_element_type=F32) * scale
    p = jnp.exp(s - jnp.max(s, axis=-1, keepdims=True))
    denom = jnp.sum(p, axis=-1, keepdims=True)
    o = jnp.dot(p.astype(BF16), v_ref[...], preferred_element_type=F32)
    o_ref[...] = (o / denom).astype(o_ref.dtype)


def _attn(q, k, v, *, batch, seq, n_mem, n_heads, name):
    t, d = q.shape
    dh = d // n_heads
    bq = _blk(seq, 1024)
    nq = seq // bq
    return pl.pallas_call(
        functools.partial(_attn_kernel, scale=dh ** -0.5),
        grid=(batch, n_heads, nq),
        in_specs=[
            pl.BlockSpec((bq, dh), lambda b, h, i: (b * nq + i, h)),
            pl.BlockSpec((n_mem, dh), lambda b, h, i: (b, h)),
            pl.BlockSpec((n_mem, dh), lambda b, h, i: (b, h)),
        ],
        out_specs=pl.BlockSpec((bq, dh), lambda b, h, i: (b * nq + i, h)),
        out_shape=jax.ShapeDtypeStruct((t, d), BF16),
        compiler_params=_params(("parallel", "parallel", "arbitrary")),
        name=name,
    )(q, k, v)


def kernel(x, mem, ffn1_norm, ffn1_w_in, ffn1_w_out, mix_norm, mix_w_in, ssm_a_re, ssm_a_im, ssm_log_dt, ssm_b_re, ssm_b_im, ssm_c_re, ssm_c_im, ssm_d, ssm_glu_w, conv_w, conv_w_out, mix_w_out, xattn_norm, mem_norm, xattn_wq, xattn_wk, xattn_wv, xattn_wo, ffn2_norm, ffn2_w_in, ffn2_w_out, final_norm):
    batch, seq, d_model = x.shape
    n_mem = mem.shape[1]
    depth = ffn1_norm.shape[0]
    d_ssm = ssm_d.shape[1]
    d_conv = conv_w.shape[2]
    n_heads = 4
    t = batch * seq
    assert seq % SSM_CHUNK == 0
    n_chunks = seq // SSM_CHUNK

    h = x.reshape(t, d_model)
    memf = mem.reshape(batch * n_mem, d_model)
    for l in range(depth):
        last = l == depth - 1
        h, (mix_w_in_b,) = _ffn(h, ffn1_norm[l], ffn1_w_in[l], ffn1_w_out[l], ffn1_norm[l], final_norm=False,
                                name="ffn1", side=(mix_w_in[l],))

        proj, (ffn2_w_in_b, ffn2_w_out_b, glu_w_b, conv_w_out_b, mix_w_out_b) = _norm_mm(
            h, mix_norm[l], mix_w_in_b, name="mix_in",
            side=(ffn2_w_in[l], ffn2_w_out[l], ssm_glu_w[l], conv_w_out[l], mix_w_out[l]))

        ys = _ssm(proj, ssm_a_re[l], ssm_a_im[l], ssm_log_dt[l], ssm_b_re[l], ssm_b_im[l],
                  ssm_c_re[l], ssm_c_im[l], ssm_d[l], n_chunks=n_chunks, name="ssm")

        cz = _conv(proj, conv_w[l], batch=batch, seq=seq, d_conv=d_conv, col0=d_ssm, name="conv")

        merged, (wq_b, wk_b, wv_b, wo_b) = _merge(
            ys, cz, glu_w_b, conv_w_out_b, proj, d_model=d_model, gate_col0=d_ssm + 3 * d_conv, name="merge",
            side=(xattn_wq[l], xattn_wk[l], xattn_wv[l], xattn_wo[l]))
        h = _mm_res(merged, mix_w_out_b, h, name="mix_out")

        q, _ = _norm_mm(h, xattn_norm[l], wq_b, name="xattn_q")
        k, _ = _norm_mm(memf, mem_norm[l], wk_b, name="xattn_k")
        v, _ = _norm_mm(memf, mem_norm[l], wv_b, name="xattn_v")
        o = _attn(q, k, v, batch=batch, seq=seq, n_mem=n_mem, n_heads=n_heads, name="xattn")
        h = _mm_res(o, wo_b, h, name="xattn_o")

        h, _ = _ffn(h, ffn2_norm[l], ffn2_w_in_b, ffn2_w_out_b, final_norm, final_norm=last, name="ffn2")
    if depth == 0:
        raise NotImplementedError("depth 0")
    return h.reshape(batch, seq, d_model)
```

```python
import functools

import jax
import jax.numpy as jnp
from jax import lax
from jax.experimental import pallas as pl
from jax.experimental.pallas import tpu as pltpu

F32 = jnp.float32
BF16 = jnp.bfloat16
RMS_EPS = 1e-6

V7X_LANES = 128
V7X_MXU_DIM = 256
V7X_VMEM_BYTES = 64 * 1024 * 1024
VMEM_LIMIT = V7X_VMEM_BYTES - 8 * 1024 * 1024

SSM_CHUNK = 16
SSM_GROUPS_PER_BLOCK = 16
SSM_SUB = 4


def _params(dims):
    return pltpu.CompilerParams(dimension_semantics=dims, vmem_limit_bytes=VMEM_LIMIT)


def _blk(dim, pref):
    b = min(dim, pref)
    assert dim % b == 0, (dim, pref)
    return b


def _rms_rows(x, g):
    ms = jnp.mean(x * x, axis=-1, keepdims=True)
    return x * lax.rsqrt(ms + RMS_EPS) * g


def _row_chunks(n_rows, chunk, body):
    chunk = min(chunk, n_rows)
    assert n_rows % chunk == 0

    def step(c, carry):
        body(pl.ds(pl.multiple_of(c * chunk, chunk), chunk))
        return carry

    lax.fori_loop(0, n_rows // chunk, step, 0)


BF16_SUBLANES = 16


def _side_spec(shape, grid):
    r, c = shape
    n_steps = grid[0] * grid[1]
    best = None
    for rb in range(BF16_SUBLANES, r + 1, BF16_SUBLANES):
        if r % rb:
            continue
        for cb in range(V7X_LANES, c + 1, V7X_LANES):
            if c % cb == 0 and (r // rb) * (c // cb) <= n_steps and (best is None or rb * cb < best[0] * best[1]):
                best = (rb, cb)
    assert best is not None, (shape, grid)
    rb, cb = best
    ncb = c // cb
    last = (r // rb) * ncb - 1

    def index(i, j):
        blk = jnp.minimum(i * grid[1] + j, last)
        return blk // ncb, blk % ncb

    return pl.BlockSpec((rb, cb), index)


def _call(body, *, grid, in_specs, out_spec, out_shape, scratch_shapes=(), args, side=(), name):
    n_in, n_side = len(in_specs), len(side)

    def kernel(*refs):
        side_in = refs[n_in:n_in + n_side]
        side_out = refs[n_in + n_side + 1:n_in + 2 * n_side + 1]

        def side_cast():
            for src, dst in zip(side_in, side_out):
                dst[...] = src[...].astype(BF16)

        body(*refs[:n_in], refs[n_in + n_side], *refs[n_in + 2 * n_side + 1:], side_cast=side_cast)

    side_specs = [_side_spec(w.shape, grid) for w in side]
    outs = pl.pallas_call(
        kernel,
        grid=grid,
        in_specs=list(in_specs) + side_specs,
        out_specs=[out_spec] + side_specs,
        out_shape=[out_shape] + [jax.ShapeDtypeStruct(w.shape, BF16) for w in side],
        scratch_shapes=list(scratch_shapes),
        compiler_params=_params(("arbitrary",) * len(grid)),
        name=name,
    )(*args, *side)
    return outs[0], tuple(outs[1:])


def _ffn_kernel(x_ref, g_ref, wa_ref, wb_ref, wout_ref, g2_ref, o_ref, xn_ref, *, final_norm, side_cast):
    f = pl.program_id(1)
    bm = x_ref.shape[0]

    @pl.when(f == 0)
    def _():
        def body(rows):
            xn_ref[rows, :] = _rms_rows(x_ref[rows, :], g_ref[...]).astype(BF16)
            o_ref[rows, :] = jnp.zeros((rows.size, o_ref.shape[1]), F32)

        _row_chunks(bm, 64, body)

    xn = xn_ref[...]
    a = jnp.dot(xn, wa_ref[...], preferred_element_type=F32)
    b = jnp.dot(xn, wb_ref[...], preferred_element_type=F32)
    gated = (a * jax.nn.sigmoid(a) * b).astype(BF16)
    side_cast()
    d = o_ref.shape[1]
    bn = min(d, 1024)
    for n0 in range(0, d, bn):
        o_ref[:, n0:n0 + bn] += jnp.dot(gated, wout_ref[:, n0:n0 + bn].astype(BF16), preferred_element_type=F32)

    @pl.when(f == pl.num_programs(1) - 1)
    def _():
        def body(rows):
            h = x_ref[rows, :] + 0.5 * o_ref[rows, :]
            if final_norm:
                h = _rms_rows(h, g2_ref[...])
            o_ref[rows, :] = h

        _row_chunks(bm, 64, body)


def _ffn(x, g, w_in, w_out, g2, *, final_norm, name, side=()):
    t, d = x.shape
    f_dim = w_out.shape[0]
    bf = _blk(f_dim, V7X_MXU_DIM)
    nf = f_dim // bf
    bm = _blk(t, 512)
    w_in = w_in.astype(BF16)
    return _call(
        functools.partial(_ffn_kernel, final_norm=final_norm),
        grid=(t // bm, nf),
        in_specs=[
            pl.BlockSpec((bm, d), lambda i, f: (i, 0)),
            pl.BlockSpec((1, d), lambda i, f: (0, 0)),
            pl.BlockSpec((d, bf), lambda i, f: (0, f)),
            pl.BlockSpec((d, bf), lambda i, f: (0, nf + f)),
            pl.BlockSpec((bf, d), lambda i, f: (f, 0)),
            pl.BlockSpec((1, d), lambda i, f: (0, 0)),
        ],
        out_spec=pl.BlockSpec((bm, d), lambda i, f: (i, 0)),
        out_shape=jax.ShapeDtypeStruct((t, d), F32),
        scratch_shapes=[pltpu.VMEM((bm, d), BF16)],
        args=(x, g.reshape(1, d), w_in, w_in, w_out, g2.reshape(1, d)),
        side=side,
        name=name,
    )


def _norm_mm_kernel(a_ref, g_ref, w_ref, o_ref, an_ref, *, side_cast):
    @pl.when(pl.program_id(1) == 0)
    def _():
        def body(rows):
            an_ref[rows, :] = _rms_rows(a_ref[rows, :], g_ref[...]).astype(BF16)

        _row_chunks(a_ref.shape[0], 64, body)

    o_ref[...] = jnp.dot(an_ref[...], w_ref[...], preferred_element_type=F32).astype(o_ref.dtype)
    side_cast()


def _norm_mm(a, g, w, *, name, bm=512, bn=1024, side=()):
    m, k = a.shape
    n = w.shape[1]
    bm = _blk(m, bm)
    bn = _blk(n, bn)
    return _call(
        _norm_mm_kernel,
        grid=(m // bm, n // bn),
        in_specs=[
            pl.BlockSpec((bm, k), lambda i, j: (i, 0)),
            pl.BlockSpec((1, k), lambda i, j: (0, 0)),
            pl.BlockSpec((k, bn), lambda i, j: (0, j)),
        ],
        out_spec=pl.BlockSpec((bm, bn), lambda i, j: (i, j)),
        out_shape=jax.ShapeDtypeStruct((m, n), BF16),
        scratch_shapes=[pltpu.VMEM((bm, k), BF16)],
        args=(a, g.reshape(1, k), w.astype(BF16)),
        side=side,
        name=name,
    )


def _mm_res_kernel(a_ref, w_ref, res_ref, o_ref):
    o_ref[...] = res_ref[...] + jnp.dot(a_ref[...], w_ref[...], preferred_element_type=F32)


def _mm_res(a, w, res, *, name, bm=1024, bn=1024):
    m, k = a.shape
    n = w.shape[1]
    bm = _blk(m, bm)
    bn = _blk(n, bn)
    return pl.pallas_call(
        _mm_res_kernel,
        grid=(m // bm, n // bn),
        in_specs=[
            pl.BlockSpec((bm, k), lambda i, j: (i, 0)),
            pl.BlockSpec((k, bn), lambda i, j: (0, j)),
            pl.BlockSpec((bm, bn), lambda i, j: (i, j)),
        ],
        out_specs=pl.BlockSpec((bm, bn), lambda i, j: (i, j)),
        out_shape=jax.ShapeDtypeStruct((m, n), F32),
        compiler_params=_params(("parallel", "arbitrary")),
        name=name,
    )(a, w.astype(BF16), res)


def _cmul(ar, ai, br, bi):
    return ar * br - ai * bi, ar * bi + ai * br


def _gelu_exact(x):
    return 0.5 * x * (1.0 + lax.erf(x * (2.0 ** -0.5)))


def _ssm_kernel(tok_ref, are_ref, aim_ref, ldt_ref, braw_ref, craw_ref, dsk_ref, o_ref,
                bstack_ref, nst_ref, dstack_ref, pw_ref, apow_ref, sre_ref, sim_ref, w_ref,
                slab_ref, u_ref, *, n_seq, n_chunks, n_log):
    w256 = V7X_MXU_DIM
    ns = braw_ref.shape[2] // 2
    n_strip = ns // V7X_LANES
    off = sre_ref.shape[2] - n_chunks
    rows = n_seq * n_chunks
    n_slab = w256 // V7X_LANES

    for hf in range(n_slab):
        slab_ref[hf] = tok_ref[:, hf * V7X_LANES:(hf + 1) * V7X_LANES].astype(F32)
    for r in range(SSM_CHUNK):
        parts = [slab_ref[hf, pl.ds(r, rows, stride=SSM_CHUNK), :] for hf in range(n_slab)]
        u_ref[r] = jnp.concatenate(parts, axis=1).astype(BF16)

    @pl.when(pl.program_id(1) == 0)
    def _prepare():
        ar = are_ref[0]
        ai = aim_ref[0]
        dt = jnp.exp(ldt_ref[0])
        mag = jnp.exp(ar * dt)
        lr = mag * jnp.cos(ai * dt)
        li = mag * jnp.sin(ai * dt)
        den = ar * ar + ai * ai
        zr = ((lr - 1.0) * ar + li * ai) / den
        zi = (li * ar - (lr - 1.0) * ai) / den
        sr, si = _cmul(braw_ref[0, :, :ns], braw_ref[0, :, ns:], zr, zi)
        cr = craw_ref[0, :, :ns]
        ci = craw_ref[0, :, ns:]
        cn = jnp.concatenate([cr, -ci], axis=1).astype(BF16)
        for j in range(SSM_CHUNK):
            bs = jnp.concatenate([sr, si], axis=1).astype(BF16)
            dj = lax.dot_general(bs, cn, (((1,), (1,)), ((), ())), preferred_element_type=F32)
            dstack_ref[(SSM_CHUNK - 1 - j) * w256:(SSM_CHUNK - j) * w256, :] = dj.astype(BF16)
            if j < SSM_SUB:
                bstack_ref[(SSM_SUB - 1 - j) * w256:(SSM_SUB - j) * w256, :] = bs
            sr, si = _cmul(sr, si, lr, li)
        mr, mi = lr, li
        for s in range(SSM_SUB):
            er, ei = _cmul(cr, ci, mr, mi)
            nst_ref[s * w256:(s + 1) * w256, :] = jnp.concatenate([er, -ei], axis=1).astype(BF16)
            if s < SSM_SUB - 1:
                mr, mi = _cmul(mr, mi, lr, li)
        pw_ref[0:1, :] = mr
        pw_ref[1:2, :] = mi
        p8r, p8i = _cmul(mr, mi, mr, mi)
        kr, ki = _cmul(p8r, p8i, p8r, p8i)
        for k in range(n_log):
            for s in range(n_strip):
                apow_ref[k, s, 0:1, :] = kr[:, s * V7X_LANES:(s + 1) * V7X_LANES]
                apow_ref[k, s, 1:2, :] = ki[:, s * V7X_LANES:(s + 1) * V7X_LANES]
            kr, ki = _cmul(kr, ki, kr, ki)
        zeros = jnp.zeros((off, V7X_LANES), F32)
        for s in range(n_strip):
            for q in range(n_seq):
                sre_ref[s, q, 0:off, :] = zeros
                sim_ref[s, q, 0:off, :] = zeros

    p4r = pw_ref[0:1, :]
    p4i = pw_ref[1:2, :]

    xr = xi = None
    for q in range(SSM_SUB):
        z = None
        for s in range(SSM_SUB):
            d = jnp.dot(u_ref[SSM_SUB * q + s], bstack_ref[s * w256:(s + 1) * w256, :],
                        preferred_element_type=F32)
            z = d if z is None else z + d
        if xr is None:
            xr, xi = z[:, :ns], z[:, ns:]
        else:
            xr, xi = _cmul(xr, xi, p4r, p4i)
            xr, xi = xr + z[:, :ns], xi + z[:, ns:]

    for s in range(n_strip):
        lanes = slice(s * V7X_LANES, (s + 1) * V7X_LANES)
        for q in range(n_seq):
            sre_ref[s, q, off:, :] = xr[q * n_chunks:(q + 1) * n_chunks, lanes]
            sim_ref[s, q, off:, :] = xi[q * n_chunks:(q + 1) * n_chunks, lanes]
    for s in range(n_strip):
        for q in range(n_seq):
            for k in range(n_log):
                sh = 1 << k
                kr = apow_ref[k, s, 0:1, :]
                ki = apow_ref[k, s, 1:2, :]
                pr = sre_ref[s, q, off - sh:off - sh + n_chunks, :]
                pi = sim_ref[s, q, off - sh:off - sh + n_chunks, :]
                tr, ti = _cmul(pr, pi, kr, ki)
                sre_ref[s, q, off:, :] = sre_ref[s, q, off:, :] + tr
                sim_ref[s, q, off:, :] = sim_ref[s, q, off:, :] + ti

    ysts = []
    for q in range(SSM_SUB):
        for s in range(n_strip):
            lanes_r = slice(s * V7X_LANES, (s + 1) * V7X_LANES)
            lanes_i = slice(ns + s * V7X_LANES, ns + (s + 1) * V7X_LANES)
            for sq in range(n_seq):
                rs = slice(sq * n_chunks, (sq + 1) * n_chunks)
                if q == 0:
                    vr = sre_ref[s, sq, off - 1:off - 1 + n_chunks, :]
                    vi = sim_ref[s, sq, off - 1:off - 1 + n_chunks, :]
                else:
                    vr, vi = _cmul(sre_ref[s, sq, off:, :], sim_ref[s, sq, off:, :],
                                   p4r[:, lanes_r], p4i[:, lanes_r])
                if q < SSM_SUB - 1:
                    sre_ref[s, sq, off:, :] = vr
                    sim_ref[s, sq, off:, :] = vi
                w_ref[rs, lanes_r] = vr.astype(BF16)
                w_ref[rs, lanes_i] = vi.astype(BF16)
        ysts.append(lax.dot_general(w_ref[...], nst_ref[...], (((1,), (1,)), ((), ())),
                                    preferred_element_type=F32))

    dsk = dsk_ref[0]
    for t in range(SSM_CHUNK):
        y = ysts[t // SSM_SUB][:, (t % SSM_SUB) * w256:(t % SSM_SUB + 1) * w256]
        y = y + dsk * u_ref[t].astype(F32)
        for r in range(t + 1):
            blk = SSM_CHUNK - 1 - t + r
            y = y + jnp.dot(u_ref[r], dstack_ref[blk * w256:(blk + 1) * w256, :],
                            preferred_element_type=F32)
        y = _gelu_exact(y)
        for hf in range(n_slab):
            slab_ref[hf, pl.ds(t, rows, stride=SSM_CHUNK), :] = y[:, hf * V7X_LANES:(hf + 1) * V7X_LANES]
    for hf in range(n_slab):
        o_ref[:, hf * V7X_LANES:(hf + 1) * V7X_LANES] = slab_ref[hf].astype(o_ref.dtype)


def _ssm(proj, a_re, a_im, log_dt, b_re, b_im, c_re, c_im, d_skip, *, n_chunks, name):
    n_tok = proj.shape[0]
    d_ssm = d_skip.shape[0]
    n_rows = n_tok // SSM_CHUNK
    n_groups, n_state = a_re.shape
    h = d_ssm // n_groups
    gpb = SSM_GROUPS_PER_BLOCK
    assert gpb * h == V7X_MXU_DIM and n_groups % gpb == 0
    n_blocks = n_groups // gpb
    ns = gpb * n_state
    n_seq = 1
    n_log = max(1, (n_chunks - 1).bit_length())
    off = max(8, 1 << (n_log - 1))

    col_group = (jnp.arange(2 * ns) % ns) // n_state
    same_group = (jnp.arange(gpb)[:, None] == col_group[None, :]).astype(F32)

    def block_diag(x_re, x_im):
        lanes = lambda x: x.reshape(n_blocks, gpb, h, n_state).transpose(0, 2, 1, 3).reshape(n_blocks, h, ns)
        x = jnp.concatenate([lanes(x_re), lanes(x_im)], axis=2)
        x = x[:, None, :, :] * same_group[None, :, None, :]
        return x.reshape(n_blocks, gpb * h, 2 * ns)

    braw = block_diag(b_re.transpose(0, 2, 1), b_im.transpose(0, 2, 1))
    craw = block_diag(c_re, c_im)
    vec = lambda x: x.reshape(n_blocks, 1, ns)
    ldt = jnp.broadcast_to(log_dt[:, None], (n_groups, n_state))
    rows = n_seq * n_chunks
    kernel = functools.partial(_ssm_kernel, n_seq=n_seq, n_chunks=n_chunks, n_log=n_log)
    vspec = pl.BlockSpec((1, 1, ns), lambda g, r: (g, 0, 0))
    mspec = pl.BlockSpec((1, V7X_MXU_DIM, 2 * ns), lambda g, r: (g, 0, 0))
    uspec = pl.BlockSpec((rows * SSM_CHUNK, V7X_MXU_DIM), lambda g, r: (r, g))
    return pl.pallas_call(
        kernel,
        grid=(n_blocks, n_rows // rows),
        in_specs=[uspec, vspec, vspec, vspec, mspec, mspec,
                  pl.BlockSpec((1, 1, V7X_MXU_DIM), lambda g, r: (g, 0, 0))],
        out_specs=uspec,
        out_shape=jax.ShapeDtypeStruct((n_tok, d_ssm), BF16),
        scratch_shapes=[
            pltpu.VMEM((SSM_SUB * V7X_MXU_DIM, 2 * ns), BF16),
            pltpu.VMEM((SSM_SUB * V7X_MXU_DIM, 2 * ns), BF16),
            pltpu.VMEM((SSM_CHUNK * V7X_MXU_DIM, V7X_MXU_DIM), BF16),
            pltpu.VMEM((8, ns), F32),
            pltpu.VMEM((n_log, ns // V7X_LANES, 8, V7X_LANES), F32),
            pltpu.VMEM((ns // V7X_LANES, n_seq, off + n_chunks, V7X_LANES), F32),
            pltpu.VMEM((ns // V7X_LANES, n_seq, off + n_chunks, V7X_LANES), F32),
            pltpu.VMEM((rows, 2 * ns), BF16),
            pltpu.VMEM((V7X_MXU_DIM // V7X_LANES, rows * SSM_CHUNK, V7X_LANES), F32),
            pltpu.VMEM((SSM_CHUNK, rows, V7X_MXU_DIM), BF16),
        ],
        compiler_params=_params(("parallel", "arbitrary")),
        name=name,
    )(proj, vec(a_re), vec(a_im), vec(ldt), braw, craw, d_skip.reshape(n_blocks, 1, V7X_MXU_DIM))


def _conv_kernel(cb_ref, cc_ref, ch_ref, w_ref, o_ref, z_ref, *, chunk):
    seq = cb_ref.shape[0]
    pad = z_ref.shape[0] - seq
    kw = w_ref.shape[0]
    z_ref[0:pad, :] = jnp.zeros((pad, z_ref.shape[1]), F32)

    def fill(rows):
        z_ref[pl.ds(rows.start + pad, rows.size), :] = cc_ref[rows, :].astype(F32) * ch_ref[rows, :].astype(F32)

    _row_chunks(seq, chunk, fill)
    for c in range(seq // chunk):
        r0 = c * chunk
        acc = None
        for k in range(kw):
            lag = kw - 1 - k
            term = w_ref[k:k + 1, :] * z_ref[pad + r0 - lag:pad + r0 - lag + chunk, :]
            acc = term if acc is None else acc + term
        o_ref[r0:r0 + chunk, :] = (cb_ref[r0:r0 + chunk, :].astype(F32) * acc).astype(o_ref.dtype)


def _conv(proj, conv_w, *, batch, seq, d_conv, col0, name):
    bc = _blk(d_conv, V7X_MXU_DIM)
    nb = d_conv // bc
    chunk = min(seq, 512)
    kw = conv_w.shape[0]
    cspec = lambda k: pl.BlockSpec((seq, bc), lambda b, j: (b, (col0 + k * d_conv) // bc + j))
    return pl.pallas_call(
        functools.partial(_conv_kernel, chunk=chunk),
        grid=(batch, nb),
        in_specs=[cspec(0), cspec(1), cspec(2), pl.BlockSpec((kw, bc), lambda b, j: (0, j))],
        out_specs=pl.BlockSpec((seq, bc), lambda b, j: (b, j)),
        out_shape=jax.ShapeDtypeStruct((batch * seq, d_conv), BF16),
        scratch_shapes=[pltpu.VMEM((8 + seq, bc), F32)],
        compiler_params=_params(("parallel", "parallel")),
        name=name,
    )(proj, proj, proj, conv_w)


def _merge_kernel(ys_ref, cz_ref, wv_ref, wg_ref, wc_ref, ga_ref, gb_ref, o_ref, *, side_cast):
    side_cast()
    ys = ys_ref[...]
    val = jnp.dot(ys, wv_ref[...], preferred_element_type=F32)
    gate = jnp.dot(ys, wg_ref[...], preferred_element_type=F32)
    yb = jnp.dot(cz_ref[...], wc_ref[...], preferred_element_type=F32)
    ya = val * jax.nn.sigmoid(gate)
    out = jax.nn.sigmoid(ga_ref[...].astype(F32)) * ya + jax.nn.sigmoid(gb_ref[...].astype(F32)) * yb
    o_ref[...] = out.astype(o_ref.dtype)


def _merge(ys, cz, glu_w, conv_w_out, proj, *, d_model, gate_col0, name, bm=1024, bn=512, side=()):
    t, k = ys.shape
    bm = _blk(t, bm)
    bn = _blk(d_model, bn)
    nj = d_model // bn
    aspec = pl.BlockSpec((bm, k), lambda i, j: (i, 0))
    wspec = lambda off: pl.BlockSpec((k, bn), lambda i, j: (0, off + j))
    gspec = lambda off: pl.BlockSpec((bm, bn), lambda i, j: (i, off + j))
    glu_w = glu_w.astype(BF16)
    return _call(
        _merge_kernel,
        grid=(t // bm, nj),
        in_specs=[aspec, aspec, wspec(0), wspec(nj), wspec(0),
                  gspec(gate_col0 // bn), gspec(gate_col0 // bn + nj)],
        out_spec=pl.BlockSpec((bm, bn), lambda i, j: (i, j)),
        out_shape=jax.ShapeDtypeStruct((t, d_model), BF16),
        args=(ys, cz, glu_w, glu_w, conv_w_out.astype(BF16), proj, proj),
        side=side,
        name=name,
    )


def _attn_kernel(q_ref, k_ref, v_ref, o_ref, *, scale):
    s = lax.dot_general(q_ref[...], k_ref[...], (((1,), (1,)), ((), ())), preferred_element_type=F32) * scale
    p = jnp.exp(s - jnp.max(s, axis=-1, keepdims=True))
    denom = jnp.sum(p, axis=-1, keepdims=True)
    o = jnp.dot(p.astype(BF16), v_ref[...], preferred_element_type=F32)
    o_ref[...] = (o / denom).astype(o_ref.dtype)


def _attn(q, k, v, *, batch, seq, n_mem, n_heads, name):
    t, d = q.shape
    dh = d // n_heads
    bq = _blk(seq, 1024)
    nq = seq // bq
    return pl.pallas_call(
        functools.partial(_attn_kernel, scale=dh ** -0.5),
        grid=(batch, n_heads, nq),
        in_specs=[
            pl.BlockSpec((bq, dh), lambda b, h, i: (b * nq + i, h)),
            pl.BlockSpec((n_mem, dh), lambda b, h, i: (b, h)),
            pl.BlockSpec((n_mem, dh), lambda b, h, i: (b, h)),
        ],
        out_specs=pl.BlockSpec((bq, dh), lambda b, h, i: (b * nq + i, h)),
        out_shape=jax.ShapeDtypeStruct((t, d), BF16),
        compiler_params=_params(("parallel", "parallel", "arbitrary")),
        name=name,
    )(q, k, v)


def kernel(x, mem, ffn1_norm, ffn1_w_in, ffn1_w_out, mix_norm, mix_w_in, ssm_a_re, ssm_a_im, ssm_log_dt, ssm_b_re, ssm_b_im, ssm_c_re, ssm_c_im, ssm_d, ssm_glu_w, conv_w, conv_w_out, mix_w_out, xattn_norm, mem_norm, xattn_wq, xattn_wk, xattn_wv, xattn_wo, ffn2_norm, ffn2_w_in, ffn2_w_out, final_norm):
    batch, seq, d_model = x.shape
    n_mem = mem.shape[1]
    depth = ffn1_norm.shape[0]
    d_ssm = ssm_d.shape[1]
    d_conv = conv_w.shape[2]
    n_heads = 4
    t = batch * seq
    assert seq % SSM_CHUNK == 0
    n_chunks = seq // SSM_CHUNK

    h = x.reshape(t, d_model)
    memf = mem.reshape(batch * n_mem, d_model)
    for l in range(depth):
        last = l == depth - 1
        h, (mix_w_in_b,) = _ffn(h, ffn1_norm[l], ffn1_w_in[l], ffn1_w_out[l], ffn1_norm[l], final_norm=False,
                                name="ffn1", side=(mix_w_in[l],))

        proj, (ffn2_w_in_b, ffn2_w_out_b, glu_w_b, conv_w_out_b, mix_w_out_b) = _norm_mm(
            h, mix_norm[l], mix_w_in_b, name="mix_in",
            side=(ffn2_w_in[l], ffn2_w_out[l], ssm_glu_w[l], conv_w_out[l], mix_w_out[l]))

        ys = _ssm(proj, ssm_a_re[l], ssm_a_im[l], ssm_log_dt[l], ssm_b_re[l], ssm_b_im[l],
                  ssm_c_re[l], ssm_c_im[l], ssm_d[l], n_chunks=n_chunks, name="ssm")

        cz = _conv(proj, conv_w[l], batch=batch, seq=seq, d_conv=d_conv, col0=d_ssm, name="conv")

        merged, (wq_b, wk_b, wv_b, wo_b) = _merge(
            ys, cz, glu_w_b, conv_w_out_b, proj, d_model=d_model, gate_col0=d_ssm + 3 * d_conv, name="merge",
            side=(xattn_wq[l], xattn_wk[l], xattn_wv[l], xattn_wo[l]))
        h = _mm_res(merged, mix_w_out_b, h, name="mix_out")

        q, _ = _norm_mm(h, xattn_norm[l], wq_b, name="xattn_q")
        k, _ = _norm_mm(memf, mem_norm[l], wk_b, name="xattn_k")
        v, _ = _norm_mm(memf, mem_norm[l], wv_b, name="xattn_v")
        o = _attn(q, k, v, batch=batch, seq=seq, n_mem=n_mem, n_heads=n_heads, name="xattn")
        h = _mm_res(o, wo_b, h, name="xattn_o")

        h, _ = _ffn(h, ffn2_norm[l], ffn2_w_in_b, ffn2_w_out_b, final_norm, final_norm=last, name="ffn2")
    if depth == 0:
        raise NotImplementedError("depth 0")
    return h.reshape(batch, seq, d_model)
```

```python
import functools

import jax
import jax.numpy as jnp
from jax import lax
from jax.experimental import pallas as pl
from jax.experimental.pallas import tpu as pltpu

F32 = jnp.float32
BF16 = jnp.bfloat16
RMS_EPS = 1e-6

V7X_LANES = 128
V7X_MXU_DIM = 256
V7X_VMEM_BYTES = 64 * 1024 * 1024
VMEM_LIMIT = V7X_VMEM_BYTES - 4 * 1024 * 1024

SSM_CHUNK = 16
SSM_GROUPS_PER_BLOCK = 16
SSM_SUB = 4


def _params(dims):
    return pltpu.CompilerParams(dimension_semantics=dims, vmem_limit_bytes=VMEM_LIMIT)


def _blk(dim, pref):
    b = min(dim, pref)
    assert dim % b == 0, (dim, pref)
    return b


def _rms_rows(x, g):
    ms = jnp.mean(x * x, axis=-1, keepdims=True)
    return x * lax.rsqrt(ms + RMS_EPS) * g


def _row_chunks(n_rows, chunk, body):
    chunk = min(chunk, n_rows)
    assert n_rows % chunk == 0

    def step(c, carry):
        body(pl.ds(pl.multiple_of(c * chunk, chunk), chunk))
        return carry

    lax.fori_loop(0, n_rows // chunk, step, 0)


BF16_SUBLANES = 16


def _side_spec(shape, grid):
    r, c = shape
    n_steps = grid[0] * grid[1]
    best = None
    for rb in range(BF16_SUBLANES, r + 1, BF16_SUBLANES):
        if r % rb:
            continue
        for cb in range(V7X_LANES, c + 1, V7X_LANES):
            if c % cb == 0 and (r // rb) * (c // cb) <= n_steps and (best is None or rb * cb < best[0] * best[1]):
                best = (rb, cb)
    assert best is not None, (shape, grid)
    rb, cb = best
    ncb = c // cb
    last = (r // rb) * ncb - 1

    def index(i, j):
        blk = jnp.minimum(i * grid[1] + j, last)
        return blk // ncb, blk % ncb

    return pl.BlockSpec((rb, cb), index)


def _call(body, *, grid, in_specs, out_specs, out_shapes, scratch_shapes=(), args, side=(), name):
    n_in, n_out, n_side = len(in_specs), len(out_specs), len(side)

    def kernel(*refs):
        side_in = refs[n_in:n_in + n_side]
        outs = refs[n_in + n_side:n_in + n_side + n_out]
        side_out = refs[n_in + n_side + n_out:n_in + 2 * n_side + n_out]

        def side_cast():
            for src, dst in zip(side_in, side_out):
                dst[...] = src[...].astype(BF16)

        body(*refs[:n_in], *outs, *refs[n_in + 2 * n_side + n_out:], side_cast=side_cast)

    side_specs = [_side_spec(w.shape, grid) for w in side]
    res = pl.pallas_call(
        kernel,
        grid=grid,
        in_specs=list(in_specs) + side_specs,
        out_specs=list(out_specs) + side_specs,
        out_shape=list(out_shapes) + [jax.ShapeDtypeStruct(w.shape, BF16) for w in side],
        scratch_shapes=list(scratch_shapes),
        compiler_params=_params(("arbitrary",) * len(grid)),
        name=name,
    )(*args, *side)
    return tuple(res[:n_out]), tuple(res[n_out:])


def _ffn_kernel(x_ref, g_ref, wa_ref, wb_ref, wout_ref, g2_ref, *rest, mode, side_cast):
    if mode == "next_norm":
        o_ref, hn_ref, xn_ref = rest
    else:
        o_ref, xn_ref = rest
    f = pl.program_id(1)
    bm = x_ref.shape[0]

    @pl.when(f == 0)
    def _():
        def body(rows):
            xn_ref[rows, :] = _rms_rows(x_ref[rows, :], g_ref[...]).astype(BF16)
            o_ref[rows, :] = jnp.zeros((rows.size, o_ref.shape[1]), F32)

        _row_chunks(bm, 64, body)

    xn = xn_ref[...]
    a = jnp.dot(xn, wa_ref[...], preferred_element_type=F32)
    b = jnp.dot(xn, wb_ref[...], preferred_element_type=F32)
    gated = (a * jax.nn.sigmoid(a) * b).astype(BF16)
    side_cast()
    d = o_ref.shape[1]
    bn = min(d, 1024)
    for n0 in range(0, d, bn):
        o_ref[:, n0:n0 + bn] += jnp.dot(gated, wout_ref[:, n0:n0 + bn], preferred_element_type=F32)

    @pl.when(f == pl.num_programs(1) - 1)
    def _():
        def body(rows):
            h = x_ref[rows, :] + 0.5 * o_ref[rows, :]
            if mode == "final_norm":
                h = _rms_rows(h, g2_ref[...])
            elif mode == "next_norm":
                hn_ref[rows, :] = _rms_rows(h, g2_ref[...]).astype(BF16)
            o_ref[rows, :] = h

        _row_chunks(bm, 64, body)


def _ffn(x, g, w_in, w_out, g2, *, mode, name, side=()):
    t, d = x.shape
    f_dim = w_out.shape[0]
    bf = _blk(f_dim, V7X_MXU_DIM)
    nf = f_dim // bf
    bm = _blk(t, 512)
    w_in = w_in.astype(BF16)
    row_spec = pl.BlockSpec((bm, d), lambda i, f: (i, 0))
    out_specs, out_shapes = [row_spec], [jax.ShapeDtypeStruct((t, d), F32)]
    if mode == "next_norm":
        out_specs, out_shapes = out_specs + [row_spec], out_shapes + [jax.ShapeDtypeStruct((t, d), BF16)]
    return _call(
        functools.partial(_ffn_kernel, mode=mode),
        grid=(t // bm, nf),
        in_specs=[
            row_spec,
            pl.BlockSpec((1, d), lambda i, f: (0, 0)),
            pl.BlockSpec((d, bf), lambda i, f: (0, f)),
            pl.BlockSpec((d, bf), lambda i, f: (0, nf + f)),
            pl.BlockSpec((bf, d), lambda i, f: (f, 0)),
            pl.BlockSpec((1, d), lambda i, f: (0, 0)),
        ],
        out_specs=out_specs,
        out_shapes=out_shapes,
        scratch_shapes=[pltpu.VMEM((bm, d), BF16)],
        args=(x, g.reshape(1, d), w_in, w_in, w_out.astype(BF16), g2.reshape(1, d)),
        side=side,
        name=name,
    )


def _mm_kernel(a_ref, w_ref, o_ref, *, side_cast):
    o_ref[...] = jnp.dot(a_ref[...], w_ref[...], preferred_element_type=F32).astype(o_ref.dtype)
    side_cast()


def _mm(a, w, *, name, bm=1024, bn=1024, side=()):
    m, k = a.shape
    n = w.shape[1]
    bm = _blk(m, bm)
    bn = _blk(n, bn)
    return _call(
        _mm_kernel,
        grid=(m // bm, n // bn),
        in_specs=[
            pl.BlockSpec((bm, k), lambda i, j: (i, 0)),
            pl.BlockSpec((k, bn), lambda i, j: (0, j)),
        ],
        out_specs=[pl.BlockSpec((bm, bn), lambda i, j: (i, j))],
        out_shapes=[jax.ShapeDtypeStruct((m, n), BF16)],
        args=(a, w.astype(BF16)),
        side=side,
        name=name,
    )


def _norm_mm_kernel(a_ref, g_ref, w_ref, o_ref, an_ref, *, side_cast):
    @pl.when(pl.program_id(1) == 0)
    def _():
        def body(rows):
            an_ref[rows, :] = _rms_rows(a_ref[rows, :], g_ref[...]).astype(BF16)

        _row_chunks(a_ref.shape[0], 64, body)

    o_ref[...] = jnp.dot(an_ref[...], w_ref[...], preferred_element_type=F32).astype(o_ref.dtype)
    side_cast()


def _norm_mm(a, g, w, *, name, bm=512, bn=1024, side=()):
    m, k = a.shape
    n = w.shape[1]
    bm = _blk(m, bm)
    bn = _blk(n, bn)
    return _call(
        _norm_mm_kernel,
        grid=(m // bm, n // bn),
        in_specs=[
            pl.BlockSpec((bm, k), lambda i, j: (i, 0)),
            pl.BlockSpec((1, k), lambda i, j: (0, 0)),
            pl.BlockSpec((k, bn), lambda i, j: (0, j)),
        ],
        out_specs=[pl.BlockSpec((bm, bn), lambda i, j: (i, j))],
        out_shapes=[jax.ShapeDtypeStruct((m, n), BF16)],
        scratch_shapes=[pltpu.VMEM((bm, k), BF16)],
        args=(a, g.reshape(1, k), w.astype(BF16)),
        side=side,
        name=name,
    )


def _mm_res_kernel(a_ref, w_ref, res_ref, o_ref):
    o_ref[...] = res_ref[...] + jnp.dot(a_ref[...], w_ref[...], preferred_element_type=F32)


def _mm_res(a, w, res, *, name, bm=1024, bn=1024):
    m, k = a.shape
    n = w.shape[1]
    bm = _blk(m, bm)
    bn = _blk(n, bn)
    return pl.pallas_call(
        _mm_res_kernel,
        grid=(m // bm, n // bn),
        in_specs=[
            pl.BlockSpec((bm, k), lambda i, j: (i, 0)),
            pl.BlockSpec((k, bn), lambda i, j: (0, j)),
            pl.BlockSpec((bm, bn), lambda i, j: (i, j)),
        ],
        out_specs=pl.BlockSpec((bm, bn), lambda i, j: (i, j)),
        out_shape=jax.ShapeDtypeStruct((m, n), F32),
        compiler_params=_params(("parallel", "arbitrary")),
        name=name,
    )(a, w.astype(BF16), res)


def _cmul(ar, ai, br, bi):
    return ar * br - ai * bi, ar * bi + ai * br


def _gelu_exact(x):
    return 0.5 * x * (1.0 + lax.erf(x * (2.0 ** -0.5)))


def _ssm_kernel(tok_ref, are_ref, aim_ref, ldt_ref, braw_ref, craw_ref, dsk_ref, o_ref,
                bstack_ref, nst_ref, dstack_ref, pw_ref, apow_ref, sre_ref, sim_ref, w_ref,
                slab_ref, u_ref, *, n_seq, n_chunks, n_log):
    w256 = V7X_MXU_DIM
    ns = braw_ref.shape[2] // 2
    n_strip = ns // V7X_LANES
    off = sre_ref.shape[2] - n_chunks
    rows = n_seq * n_chunks
    n_slab = w256 // V7X_LANES

    for hf in range(n_slab):
        slab_ref[hf] = tok_ref[:, hf * V7X_LANES:(hf + 1) * V7X_LANES].astype(F32)
    for r in range(SSM_CHUNK):
        parts = [slab_ref[hf, pl.ds(r, rows, stride=SSM_CHUNK), :] for hf in range(n_slab)]
        u_ref[r] = jnp.concatenate(parts, axis=1).astype(BF16)

    @pl.when(pl.program_id(1) == 0)
    def _prepare():
        ar = are_ref[0]
        ai = aim_ref[0]
        dt = jnp.exp(ldt_ref[0])
        mag = jnp.exp(ar * dt)
        lr = mag * jnp.cos(ai * dt)
        li = mag * jnp.sin(ai * dt)
        den = ar * ar + ai * ai
        zr = ((lr - 1.0) * ar + li * ai) / den
        zi = (li * ar - (lr - 1.0) * ai) / den
        sr, si = _cmul(braw_ref[0, :, :ns], braw_ref[0, :, ns:], zr, zi)
        cr = craw_ref[0, :, :ns]
        ci = craw_ref[0, :, ns:]
        cn = jnp.concatenate([cr, -ci], axis=1).astype(BF16)
        for j in range(SSM_CHUNK):
            bs = jnp.concatenate([sr, si], axis=1).astype(BF16)
            dj = lax.dot_general(bs, cn, (((1,), (1,)), ((), ())), preferred_element_type=F32)
            dstack_ref[(SSM_CHUNK - 1 - j) * w256:(SSM_CHUNK - j) * w256, :] = dj.astype(BF16)
            if j < SSM_SUB:
                bstack_ref[(SSM_SUB - 1 - j) * w256:(SSM_SUB - j) * w256, :] = bs
            sr, si = _cmul(sr, si, lr, li)
        mr, mi = lr, li
        for s in range(SSM_SUB):
            er, ei = _cmul(cr, ci, mr, mi)
            nst_ref[s * w256:(s + 1) * w256, :] = jnp.concatenate([er, -ei], axis=1).astype(BF16)
            if s < SSM_SUB - 1:
                mr, mi = _cmul(mr, mi, lr, li)
        pw_ref[0:1, :] = mr
        pw_ref[1:2, :] = mi
        p8r, p8i = _cmul(mr, mi, mr, mi)
        kr, ki = _cmul(p8r, p8i, p8r, p8i)
        for k in range(n_log):
            for s in range(n_strip):
                apow_ref[k, s, 0:1, :] = kr[:, s * V7X_LANES:(s + 1) * V7X_LANES]
                apow_ref[k, s, 1:2, :] = ki[:, s * V7X_LANES:(s + 1) * V7X_LANES]
            kr, ki = _cmul(kr, ki, kr, ki)
        zeros = jnp.zeros((off, V7X_LANES), F32)
        for s in range(n_strip):
            for q in range(n_seq):
                sre_ref[s, q, 0:off, :] = zeros
                sim_ref[s, q, 0:off, :] = zeros

    p4r = pw_ref[0:1, :]
    p4i = pw_ref[1:2, :]

    xr = xi = None
    for q in range(SSM_SUB):
        z = None
        for s in range(SSM_SUB):
            d = jnp.dot(u_ref[SSM_SUB * q + s], bstack_ref[s * w256:(s + 1) * w256, :],
                        preferred_element_type=F32)
            z = d if z is None else z + d
        if xr is None:
            xr, xi = z[:, :ns], z[:, ns:]
        else:
            xr, xi = _cmul(xr, xi, p4r, p4i)
            xr, xi = xr + z[:, :ns], xi + z[:, ns:]

    for s in range(n_strip):
        lanes = slice(s * V7X_LANES, (s + 1) * V7X_LANES)
        for q in range(n_seq):
            sre_ref[s, q, off:, :] = xr[q * n_chunks:(q + 1) * n_chunks, lanes]
            sim_ref[s, q, off:, :] = xi[q * n_chunks:(q + 1) * n_chunks, lanes]
    for s in range(n_strip):
        for q in range(n_seq):
            for k in range(n_log):
                sh = 1 << k
                kr = apow_ref[k, s, 0:1, :]
                ki = apow_ref[k, s, 1:2, :]
                pr = sre_ref[s, q, off - sh:off - sh + n_chunks, :]
                pi = sim_ref[s, q, off - sh:off - sh + n_chunks, :]
                tr, ti = _cmul(pr, pi, kr, ki)
                sre_ref[s, q, off:, :] = sre_ref[s, q, off:, :] + tr
                sim_ref[s, q, off:, :] = sim_ref[s, q, off:, :] + ti

    ysts = []
    for q in range(SSM_SUB):
        for s in range(n_strip):
            lanes_r = slice(s * V7X_LANES, (s + 1) * V7X_LANES)
            lanes_i = slice(ns + s * V7X_LANES, ns + (s + 1) * V7X_LANES)
            for sq in range(n_seq):
                rs = slice(sq * n_chunks, (sq + 1) * n_chunks)
                if q == 0:
                    vr = sre_ref[s, sq, off - 1:off - 1 + n_chunks, :]
                    vi = sim_ref[s, sq, off - 1:off - 1 + n_chunks, :]
                else:
                    vr, vi = _cmul(sre_ref[s, sq, off:, :], sim_ref[s, sq, off:, :],
                                   p4r[:, lanes_r], p4i[:, lanes_r])
                if q < SSM_SUB - 1:
                    sre_ref[s, sq, off:, :] = vr
                    sim_ref[s, sq, off:, :] = vi
                w_ref[rs, lanes_r] = vr.astype(BF16)
                w_ref[rs, lanes_i] = vi.astype(BF16)
        ysts.append(lax.dot_general(w_ref[...], nst_ref[...], (((1,), (1,)), ((), ())),
                                    preferred_element_type=F32))

    dsk = dsk_ref[0]
    for t in range(SSM_CHUNK):
        y = ysts[t // SSM_SUB][:, (t % SSM_SUB) * w256:(t % SSM_SUB + 1) * w256]
        y = y + dsk * u_ref[t].astype(F32)
        for r in range(t + 1):
            blk = SSM_CHUNK - 1 - t + r
            y = y + jnp.dot(u_ref[r], dstack_ref[blk * w256:(blk + 1) * w256, :],
                            preferred_element_type=F32)
        y = _gelu_exact(y)
        for hf in range(n_slab):
            slab_ref[hf, pl.ds(t, rows, stride=SSM_CHUNK), :] = y[:, hf * V7X_LANES:(hf + 1) * V7X_LANES]
    for hf in range(n_slab):
        o_ref[:, hf * V7X_LANES:(hf + 1) * V7X_LANES] = slab_ref[hf].astype(o_ref.dtype)


def _ssm(proj, a_re, a_im, log_dt, b_re, b_im, c_re, c_im, d_skip, *, n_chunks, name):
    n_tok = proj.shape[0]
    d_ssm = d_skip.shape[0]
    n_rows = n_tok // SSM_CHUNK
    n_groups, n_state = a_re.shape
    h = d_ssm // n_groups
    gpb = SSM_GROUPS_PER_BLOCK
    assert gpb * h == V7X_MXU_DIM and n_groups % gpb == 0
    n_blocks = n_groups // gpb
    ns = gpb * n_state
    n_seq = 1
    n_log = max(1, (n_chunks - 1).bit_length())
    off = max(8, 1 << (n_log - 1))

    col_group = (jnp.arange(2 * ns) % ns) // n_state
    same_group = (jnp.arange(gpb)[:, None] == col_group[None, :]).astype(F32)

    def block_diag(x_re, x_im):
        lanes = lambda x: x.reshape(n_blocks, gpb, h, n_state).transpose(0, 2, 1, 3).reshape(n_blocks, h, ns)
        x = jnp.concatenate([lanes(x_re), lanes(x_im)], axis=2)
        x = x[:, None, :, :] * same_group[None, :, None, :]
        return x.reshape(n_blocks, gpb * h, 2 * ns)

    braw = block_diag(b_re.transpose(0, 2, 1), b_im.transpose(0, 2, 1))
    craw = block_diag(c_re, c_im)
    vec = lambda x: x.reshape(n_blocks, 1, ns)
    ldt = jnp.broadcast_to(log_dt[:, None], (n_groups, n_state))
    rows = n_seq * n_chunks
    kernel = functools.partial(_ssm_kernel, n_seq=n_seq, n_chunks=n_chunks, n_log=n_log)
    vspec = pl.BlockSpec((1, 1, ns), lambda g, r: (g, 0, 0))
    mspec = pl.BlockSpec((1, V7X_MXU_DIM, 2 * ns), lambda g, r: (g, 0, 0))
    uspec = pl.BlockSpec((rows * SSM_CHUNK, V7X_MXU_DIM), lambda g, r: (r, g))
    return pl.pallas_call(
        kernel,
        grid=(n_blocks, n_rows // rows),
        in_specs=[uspec, vspec, vspec, vspec, mspec, mspec,
                  pl.BlockSpec((1, 1, V7X_MXU_DIM), lambda g, r: (g, 0, 0))],
        out_specs=uspec,
        out_shape=jax.ShapeDtypeStruct((n_tok, d_ssm), BF16),
        scratch_shapes=[
            pltpu.VMEM((SSM_SUB * V7X_MXU_DIM, 2 * ns), BF16),
            pltpu.VMEM((SSM_SUB * V7X_MXU_DIM, 2 * ns), BF16),
            pltpu.VMEM((SSM_CHUNK * V7X_MXU_DIM, V7X_MXU_DIM), BF16),
            pltpu.VMEM((8, ns), F32),
            pltpu.VMEM((n_log, ns // V7X_LANES, 8, V7X_LANES), F32),
            pltpu.VMEM((ns // V7X_LANES, n_seq, off + n_chunks, V7X_LANES), F32),
            pltpu.VMEM((ns // V7X_LANES, n_seq, off + n_chunks, V7X_LANES), F32),
            pltpu.VMEM((rows, 2 * ns), BF16),
            pltpu.VMEM((V7X_MXU_DIM // V7X_LANES, rows * SSM_CHUNK, V7X_LANES), F32),
            pltpu.VMEM((SSM_CHUNK, rows, V7X_MXU_DIM), BF16),
        ],
        compiler_params=_params(("parallel", "arbitrary")),
        name=name,
    )(proj, vec(a_re), vec(a_im), vec(ldt), braw, craw, d_skip.reshape(n_blocks, 1, V7X_MXU_DIM))


def _conv_kernel(cb_ref, cc_ref, ch_ref, w_ref, o_ref, z_ref, *, chunk):
    seq = cb_ref.shape[0]
    pad = z_ref.shape[0] - seq
    kw = w_ref.shape[0]
    z_ref[0:pad, :] = jnp.zeros((pad, z_ref.shape[1]), F32)

    def fill(rows):
        z_ref[pl.ds(rows.start + pad, rows.size), :] = cc_ref[rows, :].astype(F32) * ch_ref[rows, :].astype(F32)

    _row_chunks(seq, chunk, fill)
    for c in range(seq // chunk):
        r0 = c * chunk
        acc = None
        for k in range(kw):
            lag = kw - 1 - k
            term = w_ref[k:k + 1, :] * z_ref[pad + r0 - lag:pad + r0 - lag + chunk, :]
            acc = term if acc is None else acc + term
        o_ref[r0:r0 + chunk, :] = (cb_ref[r0:r0 + chunk, :].astype(F32) * acc).astype(o_ref.dtype)


def _conv(proj, conv_w, *, batch, seq, d_conv, col0, name):
    bc = _blk(d_conv, V7X_MXU_DIM)
    nb = d_conv // bc
    chunk = min(seq, 512)
    kw = conv_w.shape[0]
    cspec = lambda k: pl.BlockSpec((seq, bc), lambda b, j: (b, (col0 + k * d_conv) // bc + j))
    return pl.pallas_call(
        functools.partial(_conv_kernel, chunk=chunk),
        grid=(batch, nb),
        in_specs=[cspec(0), cspec(1), cspec(2), pl.BlockSpec((kw, bc), lambda b, j: (0, j))],
        out_specs=pl.BlockSpec((seq, bc), lambda b, j: (b, j)),
        out_shape=jax.ShapeDtypeStruct((batch * seq, d_conv), BF16),
        scratch_shapes=[pltpu.VMEM((8 + seq, bc), F32)],
        compiler_params=_params(("parallel", "parallel")),
        name=name,
    )(proj, proj, proj, conv_w)


def _merge_kernel(ys_ref, cz_ref, wv_ref, wg_ref, wc_ref, ga_ref, gb_ref, o_ref, *, side_cast):
    side_cast()
    ys = ys_ref[...]
    val = jnp.dot(ys, wv_ref[...], preferred_element_type=F32)
    gate = jnp.dot(ys, wg_ref[...], preferred_element_type=F32)
    yb = jnp.dot(cz_ref[...], wc_ref[...], preferred_element_type=F32)
    ya = val * jax.nn.sigmoid(gate)
    out = jax.nn.sigmoid(ga_ref[...].astype(F32)) * ya + jax.nn.sigmoid(gb_ref[...].astype(F32)) * yb
    o_ref[...] = out.astype(o_ref.dtype)


def _merge(ys, cz, glu_w, conv_w_out, proj, *, d_model, gate_col0, name, bm=1024, bn=512, side=()):
    t, k = ys.shape
    bm = _blk(t, bm)
    bn = _blk(d_model, bn)
    nj = d_model // bn
    aspec = pl.BlockSpec((bm, k), lambda i, j: (i, 0))
    wspec = lambda off: pl.BlockSpec((k, bn), lambda i, j: (0, off + j))
    gspec = lambda off: pl.BlockSpec((bm, bn), lambda i, j: (i, off + j))
    glu_w = glu_w.astype(BF16)
    return _call(
        _merge_kernel,
        grid=(t // bm, nj),
        in_specs=[aspec, aspec, wspec(0), wspec(nj), wspec(0),
                  gspec(gate_col0 // bn), gspec(gate_col0 // bn + nj)],
        out_specs=[pl.BlockSpec((bm, bn), lambda i, j: (i, j))],
        out_shapes=[jax.ShapeDtypeStruct((t, d_model), BF16)],
        args=(ys, cz, glu_w, glu_w, conv_w_out.astype(BF16), proj, proj),
        side=side,
        name=name,
    )


def _attn_kernel(q_ref, k_ref, v_ref, o_ref, *, scale):
    s = lax.dot_general(q_ref[...], k_ref[...], (((1,), (1,)), ((), ())), preferred_element_type=F32) * scale
    p = jnp.exp(s - jnp.max(s, axis=-1, keepdims=True))
    denom = jnp.sum(p, axis=-1, keepdims=True)
    o = jnp.dot(p.astype(BF16), v_ref[...], preferred_element_type=F32)
    o_ref[...] = (o / denom).astype(o_ref.dtype)


def _attn(q, k, v, *, batch, seq, n_mem, n_heads, name):
    t, d = q.shape
    dh = d // n_heads
    bq = _blk(seq, 1024)
    nq = seq // bq
    return pl.pallas_call(
        functools.partial(_attn_kernel, scale=dh ** -0.5),
        grid=(batch, n_heads, nq),
        in_specs=[
            pl.BlockSpec((bq, dh), lambda b, h, i: (b * nq + i, h)),
            pl.BlockSpec((n_mem, dh), lambda b, h, i: (b, h)),
            pl.BlockSpec((n_mem, dh), lambda b, h, i: (b, h)),
        ],
        out_specs=pl.BlockSpec((bq, dh), lambda b, h, i: (b * nq + i, h)),
        out_shape=jax.ShapeDtypeStruct((t, d), BF16),
        compiler_params=_params(("parallel", "parallel", "arbitrary")),
        name=name,
    )(q, k, v)


def kernel(x, mem, ffn1_norm, ffn1_w_in, ffn1_w_out, mix_norm, mix_w_in, ssm_a_re, ssm_a_im, ssm_log_dt, ssm_b_re, ssm_b_im, ssm_c_re, ssm_c_im, ssm_d, ssm_glu_w, conv_w, conv_w_out, mix_w_out, xattn_norm, mem_norm, xattn_wq, xattn_wk, xattn_wv, xattn_wo, ffn2_norm, ffn2_w_in, ffn2_w_out, final_norm):
    batch, seq, d_model = x.shape
    n_mem = mem.shape[1]
    depth = ffn1_norm.shape[0]
    d_ssm = ssm_d.shape[1]
    d_conv = conv_w.shape[2]
    n_heads = 4
    t = batch * seq
    assert seq % SSM_CHUNK == 0
    n_chunks = seq // SSM_CHUNK

    h = x.reshape(t, d_model)
    memf = mem.reshape(batch * n_mem, d_model)
    for l in range(depth):
        last = l == depth - 1
        (h, un), (mix_w_in_b,) = _ffn(h, ffn1_norm[l], ffn1_w_in[l], ffn1_w_out[l], mix_norm[l], mode="next_norm",
                                      name="ffn1", side=(mix_w_in[l],))

        (proj,), (ffn2_w_in_b, ffn2_w_out_b, glu_w_b, conv_w_out_b, mix_w_out_b) = _mm(
            un, mix_w_in_b, name="mix_in",
            side=(ffn2_w_in[l], ffn2_w_out[l], ssm_glu_w[l], conv_w_out[l], mix_w_out[l]))

        ys = _ssm(proj, ssm_a_re[l], ssm_a_im[l], ssm_log_dt[l], ssm_b_re[l], ssm_b_im[l],
                  ssm_c_re[l], ssm_c_im[l], ssm_d[l], n_chunks=n_chunks, name="ssm")

        cz = _conv(proj, conv_w[l], batch=batch, seq=seq, d_conv=d_conv, col0=d_ssm, name="conv")

        (merged,), (wq_b, wk_b, wv_b, wo_b) = _merge(
            ys, cz, glu_w_b, conv_w_out_b, proj, d_model=d_model, gate_col0=d_ssm + 3 * d_conv, name="merge",
            side=(xattn_wq[l], xattn_wk[l], xattn_wv[l], xattn_wo[l]))
        h = _mm_res(merged, mix_w_out_b, h, name="mix_out")

        (q,), _ = _norm_mm(h, xattn_norm[l], wq_b, name="xattn_q")
        (k,), _ = _norm_mm(memf, mem_norm[l], wk_b, name="xattn_k")
        (v,), _ = _norm_mm(memf, mem_norm[l], wv_b, name="xattn_v")
        o = _attn(q, k, v, batch=batch, seq=seq, n_mem=n_mem, n_heads=n_heads, name="xattn")
        h = _mm_res(o, wo_b, h, name="xattn_o")

        (h,), _ = _ffn(h, ffn2_norm[l], ffn2_w_in_b, ffn2_w_out_b, final_norm,
                       mode="final_norm" if last else "plain", name="ffn2")
    if depth == 0:
        raise NotImplementedError("depth 0")
    return h.reshape(batch, seq, d_model)
```

```python
import functools

import jax
import jax.numpy as jnp
from jax import lax
from jax.experimental import pallas as pl
from jax.experimental.pallas import tpu as pltpu

F32 = jnp.float32
BF16 = jnp.bfloat16
RMS_EPS = 1e-6

V7X_LANES = 128
V7X_MXU_DIM = 256
V7X_VMEM_BYTES = 64 * 1024 * 1024
VMEM_LIMIT = V7X_VMEM_BYTES - 4 * 1024 * 1024

SSM_CHUNK = 16
SSM_GROUPS_PER_BLOCK = 16
SSM_SUB = 4


def _params(dims):
    return pltpu.CompilerParams(dimension_semantics=dims, vmem_limit_bytes=VMEM_LIMIT)


def _blk(dim, pref):
    b = min(dim, pref)
    assert dim % b == 0, (dim, pref)
    return b


def _rms_rows(x, g):
    ms = jnp.mean(x * x, axis=-1, keepdims=True)
    return x * lax.rsqrt(ms + RMS_EPS) * g


def _row_chunks(n_rows, chunk, body):
    chunk = min(chunk, n_rows)
    assert n_rows % chunk == 0

    def step(c, carry):
        body(pl.ds(pl.multiple_of(c * chunk, chunk), chunk))
        return carry

    lax.fori_loop(0, n_rows // chunk, step, 0)


BF16_SUBLANES = 16


def _side_spec(shape, grid):
    r, c = shape
    n_steps = grid[0] * grid[1]
    best = None
    for rb in range(BF16_SUBLANES, r + 1, BF16_SUBLANES):
        if r % rb:
            continue
        for cb in range(V7X_LANES, c + 1, V7X_LANES):
            if c % cb == 0 and (r // rb) * (c // cb) <= n_steps and (best is None or rb * cb < best[0] * best[1]):
                best = (rb, cb)
    assert best is not None, (shape, grid)
    rb, cb = best
    ncb = c // cb
    last = (r // rb) * ncb - 1

    def index(i, j):
        blk = jnp.minimum(i * grid[1] + j, last)
        return blk // ncb, blk % ncb

    return pl.BlockSpec((rb, cb), index)


def _call(body, *, grid, in_specs, out_specs, out_shapes, scratch_shapes=(), args, side=(), name):
    n_in, n_out, n_side = len(in_specs), len(out_specs), len(side)

    def kernel(*refs):
        side_in = refs[n_in:n_in + n_side]
        outs = refs[n_in + n_side:n_in + n_side + n_out]
        side_out = refs[n_in + n_side + n_out:n_in + 2 * n_side + n_out]

        def side_cast():
            for src, dst in zip(side_in, side_out):
                dst[...] = src[...].astype(BF16)

        body(*refs[:n_in], *outs, *refs[n_in + 2 * n_side + n_out:], side_cast=side_cast)

    side_specs = [_side_spec(w.shape, grid) for w in side]
    res = pl.pallas_call(
        kernel,
        grid=grid,
        in_specs=list(in_specs) + side_specs,
        out_specs=list(out_specs) + side_specs,
        out_shape=list(out_shapes) + [jax.ShapeDtypeStruct(w.shape, BF16) for w in side],
        scratch_shapes=list(scratch_shapes),
        compiler_params=_params(("arbitrary",) * len(grid)),
        name=name,
    )(*args, *side)
    return tuple(res[:n_out]), tuple(res[n_out:])


def _ffn_kernel(x_ref, g_ref, wa_ref, wb_ref, wout_ref, g2_ref, *rest, mode, side_cast):
    if mode == "next_norm":
        o_ref, hn_ref, xn_ref = rest
    else:
        o_ref, xn_ref = rest
    f = pl.program_id(1)
    bm = x_ref.shape[0]

    @pl.when(f == 0)
    def _():
        def body(rows):
            xn_ref[rows, :] = _rms_rows(x_ref[rows, :], g_ref[...]).astype(BF16)
            o_ref[rows, :] = jnp.zeros((rows.size, o_ref.shape[1]), F32)

        _row_chunks(bm, 64, body)

    xn = xn_ref[...]
    a = jnp.dot(xn, wa_ref[...], preferred_element_type=F32)
    b = jnp.dot(xn, wb_ref[...], preferred_element_type=F32)
    gated = (a * jax.nn.sigmoid(a) * b).astype(BF16)
    side_cast()
    d = o_ref.shape[1]
    bn = min(d, 1024)
    for n0 in range(0, d, bn):
        o_ref[:, n0:n0 + bn] += jnp.dot(gated, wout_ref[:, n0:n0 + bn], preferred_element_type=F32)

    @pl.when(f == pl.num_programs(1) - 1)
    def _():
        def body(rows):
            h = x_ref[rows, :] + 0.5 * o_ref[rows, :]
            if mode == "final_norm":
                h = _rms_rows(h, g2_ref[...])
            elif mode == "next_norm":
                hn_ref[rows, :] = _rms_rows(h, g2_ref[...]).astype(BF16)
            o_ref[rows, :] = h

        _row_chunks(bm, 64, body)


def _ffn(x, g, w_in, w_out, g2, *, mode, name, side=()):
    t, d = x.shape
    f_dim = w_out.shape[0]
    bf = _blk(f_dim, V7X_MXU_DIM)
    nf = f_dim // bf
    bm = _blk(t, 512)
    w_in = w_in.astype(BF16)
    row_spec = pl.BlockSpec((bm, d), lambda i, f: (i, 0))
    out_specs, out_shapes = [row_spec], [jax.ShapeDtypeStruct((t, d), F32)]
    if mode == "next_norm":
        out_specs, out_shapes = out_specs + [row_spec], out_shapes + [jax.ShapeDtypeStruct((t, d), BF16)]
    return _call(
        functools.partial(_ffn_kernel, mode=mode),
        grid=(t // bm, nf),
        in_specs=[
            row_spec,
            pl.BlockSpec((1, d), lambda i, f: (0, 0)),
            pl.BlockSpec((d, bf), lambda i, f: (0, f)),
            pl.BlockSpec((d, bf), lambda i, f: (0, nf + f)),
            pl.BlockSpec((bf, d), lambda i, f: (f, 0)),
            pl.BlockSpec((1, d), lambda i, f: (0, 0)),
        ],
        out_specs=out_specs,
        out_shapes=out_shapes,
        scratch_shapes=[pltpu.VMEM((bm, d), BF16)],
        args=(x, g.reshape(1, d), w_in, w_in, w_out.astype(BF16), g2.reshape(1, d)),
        side=side,
        name=name,
    )


def _mm_kernel(a_ref, w_ref, o_ref, *, side_cast):
    o_ref[...] = jnp.dot(a_ref[...], w_ref[...], preferred_element_type=F32).astype(o_ref.dtype)
    side_cast()


def _mm(a, w, *, name, bm=1024, bn=1024, side=()):
    m, k = a.shape
    n = w.shape[1]
    bm = _blk(m, bm)
    bn = _blk(n, bn)
    return _call(
        _mm_kernel,
        grid=(m // bm, n // bn),
        in_specs=[
            pl.BlockSpec((bm, k), lambda i, j: (i, 0)),
            pl.BlockSpec((k, bn), lambda i, j: (0, j)),
        ],
        out_specs=[pl.BlockSpec((bm, bn), lambda i, j: (i, j))],
        out_shapes=[jax.ShapeDtypeStruct((m, n), BF16)],
        args=(a, w.astype(BF16)),
        side=side,
        name=name,
    )


def _norm_mm_kernel(a_ref, g_ref, w_ref, o_ref, an_ref, *, side_cast):
    @pl.when(pl.program_id(1) == 0)
    def _():
        def body(rows):
            an_ref[rows, :] = _rms_rows(a_ref[rows, :].astype(F32), g_ref[...]).astype(BF16)

        _row_chunks(a_ref.shape[0], 64, body)

    o_ref[...] = jnp.dot(an_ref[...], w_ref[...], preferred_element_type=F32).astype(o_ref.dtype)
    side_cast()


def _norm_mm(a, g, w, *, name, bm=512, bn=1024, side=()):
    m, k = a.shape
    n = w.shape[1]
    bm = _blk(m, bm)
    bn = _blk(n, bn)
    return _call(
        _norm_mm_kernel,
        grid=(m // bm, n // bn),
        in_specs=[
            pl.BlockSpec((bm, k), lambda i, j: (i, 0)),
            pl.BlockSpec((1, k), lambda i, j: (0, 0)),
            pl.BlockSpec((k, bn), lambda i, j: (0, j)),
        ],
        out_specs=[pl.BlockSpec((bm, bn), lambda i, j: (i, j))],
        out_shapes=[jax.ShapeDtypeStruct((m, n), BF16)],
        scratch_shapes=[pltpu.VMEM((bm, k), BF16)],
        args=(a, g.reshape(1, k), w.astype(BF16)),
        side=side,
        name=name,
    )


def _mm_res_kernel(a_ref, w_ref, res_ref, o_ref, *copy_ref):
    bn = o_ref.shape[1]
    cn = min(bn, 2 * V7X_MXU_DIM)
    for n0 in range(0, bn, cn):
        cols = slice(n0, n0 + cn)
        out = res_ref[:, cols] + jnp.dot(a_ref[...], w_ref[:, cols], preferred_element_type=F32)
        o_ref[:, cols] = out
        for ref in copy_ref:
            ref[:, cols] = out.astype(ref.dtype)


def _mm_res(a, w, res, *, name, bm=1024, bn=1024, bf16_copy=False):
    m, k = a.shape
    n = w.shape[1]
    bm = _blk(m, bm)
    bn = _blk(n, bn)
    tile = pl.BlockSpec((bm, bn), lambda i, j: (i, j))
    return pl.pallas_call(
        _mm_res_kernel,
        grid=(m // bm, n // bn),
        in_specs=[
            pl.BlockSpec((bm, k), lambda i, j: (i, 0)),
            pl.BlockSpec((k, bn), lambda i, j: (0, j)),
            tile,
        ],
        out_specs=[tile] + [tile] * bf16_copy,
        out_shape=[jax.ShapeDtypeStruct((m, n), F32)] + [jax.ShapeDtypeStruct((m, n), BF16)] * bf16_copy,
        compiler_params=_params(("parallel", "arbitrary")),
        name=name,
    )(a, w.astype(BF16), res)


def _cmul(ar, ai, br, bi):
    return ar * br - ai * bi, ar * bi + ai * br


def _gelu_exact(x):
    return 0.5 * x * (1.0 + lax.erf(x * (2.0 ** -0.5)))


def _ssm_kernel(tok_ref, are_ref, aim_ref, ldt_ref, braw_ref, craw_ref, dsk_ref, o_ref,
                bstack_ref, nst_ref, dstack_ref, pw_ref, apow_ref, sre_ref, sim_ref, w_ref,
                slab_ref, u_ref, *, n_seq, n_chunks, n_log):
    w256 = V7X_MXU_DIM
    ns = braw_ref.shape[2] // 2
    n_strip = ns // V7X_LANES
    off = sre_ref.shape[2] - n_chunks
    rows = n_seq * n_chunks
    n_slab = w256 // V7X_LANES

    for hf in range(n_slab):
        slab_ref[hf] = tok_ref[:, hf * V7X_LANES:(hf + 1) * V7X_LANES].astype(F32)
    for r in range(SSM_CHUNK):
        parts = [slab_ref[hf, pl.ds(r, rows, stride=SSM_CHUNK), :] for hf in range(n_slab)]
        u_ref[r] = jnp.concatenate(parts, axis=1).astype(BF16)

    @pl.when(pl.program_id(1) == 0)
    def _prepare():
        ar = are_ref[0]
        ai = aim_ref[0]
        dt = jnp.exp(ldt_ref[0])
        mag = jnp.exp(ar * dt)
        lr = mag * jnp.cos(ai * dt)
        li = mag * jnp.sin(ai * dt)
        den = ar * ar + ai * ai
        zr = ((lr - 1.0) * ar + li * ai) / den
        zi = (li * ar - (lr - 1.0) * ai) / den
        sr, si = _cmul(braw_ref[0, :, :ns], braw_ref[0, :, ns:], zr, zi)
        cr = craw_ref[0, :, :ns]
        ci = craw_ref[0, :, ns:]
        cn = jnp.concatenate([cr, -ci], axis=1).astype(BF16)
        for j in range(SSM_CHUNK):
            bs = jnp.concatenate([sr, si], axis=1).astype(BF16)
            dj = lax.dot_general(bs, cn, (((1,), (1,)), ((), ())), preferred_element_type=F32)
            dstack_ref[(SSM_CHUNK - 1 - j) * w256:(SSM_CHUNK - j) * w256, :] = dj.astype(BF16)
            if j < SSM_SUB:
                bstack_ref[(SSM_SUB - 1 - j) * w256:(SSM_SUB - j) * w256, :] = bs
            sr, si = _cmul(sr, si, lr, li)
        mr, mi = lr, li
        for s in range(SSM_SUB):
            er, ei = _cmul(cr, ci, mr, mi)
            nst_ref[s * w256:(s + 1) * w256, :] = jnp.concatenate([er, -ei], axis=1).astype(BF16)
            if s < SSM_SUB - 1:
                mr, mi = _cmul(mr, mi, lr, li)
        pw_ref[0:1, :] = mr
        pw_ref[1:2, :] = mi
        p8r, p8i = _cmul(mr, mi, mr, mi)
        kr, ki = _cmul(p8r, p8i, p8r, p8i)
        for k in range(n_log):
            for s in range(n_strip):
                apow_ref[k, s, 0:1, :] = kr[:, s * V7X_LANES:(s + 1) * V7X_LANES]
                apow_ref[k, s, 1:2, :] = ki[:, s * V7X_LANES:(s + 1) * V7X_LANES]
            kr, ki = _cmul(kr, ki, kr, ki)
        zeros = jnp.zeros((off, V7X_LANES), F32)
        for s in range(n_strip):
            for q in range(n_seq):
                sre_ref[s, q, 0:off, :] = zeros
                sim_ref[s, q, 0:off, :] = zeros

    p4r = pw_ref[0:1, :]
    p4i = pw_ref[1:2, :]

    xr = xi = None
    for q in range(SSM_SUB):
        z = None
        for s in range(SSM_SUB):
            d = jnp.dot(u_ref[SSM_SUB * q + s], bstack_ref[s * w256:(s + 1) * w256, :],
                        preferred_element_type=F32)
            z = d if z is None else z + d
        if xr is None:
            xr, xi = z[:, :ns], z[:, ns:]
        else:
            xr, xi = _cmul(xr, xi, p4r, p4i)
            xr, xi = xr + z[:, :ns], xi + z[:, ns:]

    for s in range(n_strip):
        lanes = slice(s * V7X_LANES, (s + 1) * V7X_LANES)
        for q in range(n_seq):
            sre_ref[s, q, off:, :] = xr[q * n_chunks:(q + 1) * n_chunks, lanes]
            sim_ref[s, q, off:, :] = xi[q * n_chunks:(q + 1) * n_chunks, lanes]
    for s in range(n_strip):
        for q in range(n_seq):
            for k in range(n_log):
                sh = 1 << k
                kr = apow_ref[k, s, 0:1, :]
                ki = apow_ref[k, s, 1:2, :]
                pr = sre_ref[s, q, off - sh:off - sh + n_chunks, :]
                pi = sim_ref[s, q, off - sh:off - sh + n_chunks, :]
                tr, ti = _cmul(pr, pi, kr, ki)
                sre_ref[s, q, off:, :] = sre_ref[s, q, off:, :] + tr
                sim_ref[s, q, off:, :] = sim_ref[s, q, off:, :] + ti

    ysts = []
    for q in range(SSM_SUB):
        for s in range(n_strip):
            lanes_r = slice(s * V7X_LANES, (s + 1) * V7X_LANES)
            lanes_i = slice(ns + s * V7X_LANES, ns + (s + 1) * V7X_LANES)
            for sq in range(n_seq):
                rs = slice(sq * n_chunks, (sq + 1) * n_chunks)
                if q == 0:
                    vr = sre_ref[s, sq, off - 1:off - 1 + n_chunks, :]
                    vi = sim_ref[s, sq, off - 1:off - 1 + n_chunks, :]
                else:
                    vr, vi = _cmul(sre_ref[s, sq, off:, :], sim_ref[s, sq, off:, :],
                                   p4r[:, lanes_r], p4i[:, lanes_r])
                if q < SSM_SUB - 1:
                    sre_ref[s, sq, off:, :] = vr
                    sim_ref[s, sq, off:, :] = vi
                w_ref[rs, lanes_r] = vr.astype(BF16)
                w_ref[rs, lanes_i] = vi.astype(BF16)
        ysts.append(lax.dot_general(w_ref[...], nst_ref[...], (((1,), (1,)), ((), ())),
                                    preferred_element_type=F32))

    dsk = dsk_ref[0]
    for t in range(SSM_CHUNK):
        y = ysts[t // SSM_SUB][:, (t % SSM_SUB) * w256:(t % SSM_SUB + 1) * w256]
        y = y + dsk * u_ref[t].astype(F32)
        for r in range(t + 1):
            blk = SSM_CHUNK - 1 - t + r
            y = y + jnp.dot(u_ref[r], dstack_ref[blk * w256:(blk + 1) * w256, :],
                            preferred_element_type=F32)
        y = _gelu_exact(y)
        for hf in range(n_slab):
            slab_ref[hf, pl.ds(t, rows, stride=SSM_CHUNK), :] = y[:, hf * V7X_LANES:(hf + 1) * V7X_LANES]
    for hf in range(n_slab):
        o_ref[:, hf * V7X_LANES:(hf + 1) * V7X_LANES] = slab_ref[hf].astype(o_ref.dtype)


def _ssm(proj, a_re, a_im, log_dt, b_re, b_im, c_re, c_im, d_skip, *, n_chunks, name):
    n_tok = proj.shape[0]
    d_ssm = d_skip.shape[0]
    n_rows = n_tok // SSM_CHUNK
    n_groups, n_state = a_re.shape
    h = d_ssm // n_groups
    gpb = SSM_GROUPS_PER_BLOCK
    assert gpb * h == V7X_MXU_DIM and n_groups % gpb == 0
    n_blocks = n_groups // gpb
    ns = gpb * n_state
    n_seq = 1
    n_log = max(1, (n_chunks - 1).bit_length())
    off = max(8, 1 << (n_log - 1))

    col_group = (jnp.arange(2 * ns) % ns) // n_state
    same_group = (jnp.arange(gpb)[:, None] == col_group[None, :]).astype(F32)

    def block_diag(x_re, x_im):
        lanes = lambda x: x.reshape(n_blocks, gpb, h, n_state).transpose(0, 2, 1, 3).reshape(n_blocks, h, ns)
        x = jnp.concatenate([lanes(x_re), lanes(x_im)], axis=2)
        x = x[:, None, :, :] * same_group[None, :, None, :]
        return x.reshape(n_blocks, gpb * h, 2 * ns)

    braw = block_diag(b_re.transpose(0, 2, 1), b_im.transpose(0, 2, 1))
    craw = block_diag(c_re, c_im)
    vec = lambda x: x.reshape(n_blocks, 1, ns)
    ldt = jnp.broadcast_to(log_dt[:, None], (n_groups, n_state))
    rows = n_seq * n_chunks
    kernel = functools.partial(_ssm_kernel, n_seq=n_seq, n_chunks=n_chunks, n_log=n_log)
    vspec = pl.BlockSpec((1, 1, ns), lambda g, r: (g, 0, 0))
    mspec = pl.BlockSpec((1, V7X_MXU_DIM, 2 * ns), lambda g, r: (g, 0, 0))
    uspec = pl.BlockSpec((rows * SSM_CHUNK, V7X_MXU_DIM), lambda g, r: (r, g))
    return pl.pallas_call(
        kernel,
        grid=(n_blocks, n_rows // rows),
        in_specs=[uspec, vspec, vspec, vspec, mspec, mspec,
                  pl.BlockSpec((1, 1, V7X_MXU_DIM), lambda g, r: (g, 0, 0))],
        out_specs=uspec,
        out_shape=jax.ShapeDtypeStruct((n_tok, d_ssm), BF16),
        scratch_shapes=[
            pltpu.VMEM((SSM_SUB * V7X_MXU_DIM, 2 * ns), BF16),
            pltpu.VMEM((SSM_SUB * V7X_MXU_DIM, 2 * ns), BF16),
            pltpu.VMEM((SSM_CHUNK * V7X_MXU_DIM, V7X_MXU_DIM), BF16),
            pltpu.VMEM((8, ns), F32),
            pltpu.VMEM((n_log, ns // V7X_LANES, 8, V7X_LANES), F32),
            pltpu.VMEM((ns // V7X_LANES, n_seq, off + n_chunks, V7X_LANES), F32),
            pltpu.VMEM((ns // V7X_LANES, n_seq, off + n_chunks, V7X_LANES), F32),
            pltpu.VMEM((rows, 2 * ns), BF16),
            pltpu.VMEM((V7X_MXU_DIM // V7X_LANES, rows * SSM_CHUNK, V7X_LANES), F32),
            pltpu.VMEM((SSM_CHUNK, rows, V7X_MXU_DIM), BF16),
        ],
        compiler_params=_params(("parallel", "arbitrary")),
        name=name,
    )(proj, vec(a_re), vec(a_im), vec(ldt), braw, craw, d_skip.reshape(n_blocks, 1, V7X_MXU_DIM))


def _conv_kernel(cb_ref, cc_ref, ch_ref, w_ref, o_ref, z_ref, *, chunk):
    seq = cb_ref.shape[0]
    pad = z_ref.shape[0] - seq
    kw = w_ref.shape[0]
    z_ref[0:pad, :] = jnp.zeros((pad, z_ref.shape[1]), F32)

    def fill(rows):
        z_ref[pl.ds(rows.start + pad, rows.size), :] = cc_ref[rows, :].astype(F32) * ch_ref[rows, :].astype(F32)

    _row_chunks(seq, chunk, fill)
    for c in range(seq // chunk):
        r0 = c * chunk
        acc = None
        for k in range(kw):
            lag = kw - 1 - k
            term = w_ref[k:k + 1, :] * z_ref[pad + r0 - lag:pad + r0 - lag + chunk, :]
            acc = term if acc is None else acc + term
        o_ref[r0:r0 + chunk, :] = (cb_ref[r0:r0 + chunk, :].astype(F32) * acc).astype(o_ref.dtype)


def _conv(proj, conv_w, *, batch, seq, d_conv, col0, name):
    bc = _blk(d_conv, V7X_MXU_DIM)
    nb = d_conv // bc
    chunk = min(seq, 512)
    kw = conv_w.shape[0]
    cspec = lambda k: pl.BlockSpec((seq, bc), lambda b, j: (b, (col0 + k * d_conv) // bc + j))
    return pl.pallas_call(
        functools.partial(_conv_kernel, chunk=chunk),
        grid=(batch, nb),
        in_specs=[cspec(0), cspec(1), cspec(2), pl.BlockSpec((kw, bc), lambda b, j: (0, j))],
        out_specs=pl.BlockSpec((seq, bc), lambda b, j: (b, j)),
        out_shape=jax.ShapeDtypeStruct((batch * seq, d_conv), BF16),
        scratch_shapes=[pltpu.VMEM((8 + seq, bc), F32)],
        compiler_params=_params(("parallel", "parallel")),
        name=name,
    )(proj, proj, proj, conv_w)


def _merge_kernel(ys_ref, cz_ref, wv_ref, wg_ref, wc_ref, ga_ref, gb_ref, o_ref, *, side_cast):
    side_cast()
    ys = ys_ref[...]
    val = jnp.dot(ys, wv_ref[...], preferred_element_type=F32)
    gate = jnp.dot(ys, wg_ref[...], preferred_element_type=F32)
    yb = jnp.dot(cz_ref[...], wc_ref[...], preferred_element_type=F32)
    ya = val * jax.nn.sigmoid(gate)
    out = jax.nn.sigmoid(ga_ref[...].astype(F32)) * ya + jax.nn.sigmoid(gb_ref[...].astype(F32)) * yb
    o_ref[...] = out.astype(o_ref.dtype)


def _merge(ys, cz, glu_w, conv_w_out, proj, *, d_model, gate_col0, name, bm=1024, bn=512, side=()):
    t, k = ys.shape
    bm = _blk(t, bm)
    bn = _blk(d_model, bn)
    nj = d_model // bn
    aspec = pl.BlockSpec((bm, k), lambda i, j: (i, 0))
    wspec = lambda off: pl.BlockSpec((k, bn), lambda i, j: (0, off + j))
    gspec = lambda off: pl.BlockSpec((bm, bn), lambda i, j: (i, off + j))
    glu_w = glu_w.astype(BF16)
    return _call(
        _merge_kernel,
        grid=(t // bm, nj),
        in_specs=[aspec, aspec, wspec(0), wspec(nj), wspec(0),
                  gspec(gate_col0 // bn), gspec(gate_col0 // bn + nj)],
        out_specs=[pl.BlockSpec((bm, bn), lambda i, j: (i, j))],
        out_shapes=[jax.ShapeDtypeStruct((t, d_model), BF16)],
        args=(ys, cz, glu_w, glu_w, conv_w_out.astype(BF16), proj, proj),
        side=side,
        name=name,
    )


def _attn_kernel(q_ref, k_ref, v_ref, o_ref, *, scale):
    s = lax.dot_general(q_ref[...], k_ref[...], (((1,), (1,)), ((), ())), preferred_element_type=F32) * scale
    p = jnp.exp(s - jnp.max(s, axis=-1, keepdims=True))
    denom = jnp.sum(p, axis=-1, keepdims=True)
    o = jnp.dot(p.astype(BF16), v_ref[...], preferred_element_type=F32)
    o_ref[...] = (o / denom).astype(o_ref.dtype)


def _attn(q, k, v, *, batch, seq, n_mem, n_heads, name):
    t, d = q.shape
    dh = d // n_heads
    bq = _blk(seq, 1024)
    nq = seq // bq
    return pl.pallas_call(
        functools.partial(_attn_kernel, scale=dh ** -0.5),
        grid=(batch, n_heads, nq),
        in_specs=[
            pl.BlockSpec((bq, dh), lambda b, h, i: (b * nq + i, h)),
            pl.BlockSpec((n_mem, dh), lambda b, h, i: (b, h)),
            pl.BlockSpec((n_mem, dh), lambda b, h, i: (b, h)),
        ],
        out_specs=pl.BlockSpec((bq, dh), lambda b, h, i: (b * nq + i, h)),
        out_shape=jax.ShapeDtypeStruct((t, d), BF16),
        compiler_params=_params(("parallel", "parallel", "arbitrary")),
        name=name,
    )(q, k, v)


def kernel(x, mem, ffn1_norm, ffn1_w_in, ffn1_w_out, mix_norm, mix_w_in, ssm_a_re, ssm_a_im, ssm_log_dt, ssm_b_re, ssm_b_im, ssm_c_re, ssm_c_im, ssm_d, ssm_glu_w, conv_w, conv_w_out, mix_w_out, xattn_norm, mem_norm, xattn_wq, xattn_wk, xattn_wv, xattn_wo, ffn2_norm, ffn2_w_in, ffn2_w_out, final_norm):
    batch, seq, d_model = x.shape
    n_mem = mem.shape[1]
    depth = ffn1_norm.shape[0]
    d_ssm = ssm_d.shape[1]
    d_conv = conv_w.shape[2]
    n_heads = 4
    t = batch * seq
    assert seq % SSM_CHUNK == 0
    n_chunks = seq // SSM_CHUNK

    h = x.reshape(t, d_model)
    memf = mem.reshape(batch * n_mem, d_model)
    for l in range(depth):
        last = l == depth - 1
        (h, un), (mix_w_in_b,) = _ffn(h, ffn1_norm[l], ffn1_w_in[l], ffn1_w_out[l], mix_norm[l], mode="next_norm",
                                      name="ffn1", side=(mix_w_in[l],))

        (proj,), (ffn2_w_in_b, ffn2_w_out_b, glu_w_b, conv_w_out_b, mix_w_out_b) = _mm(
            un, mix_w_in_b, name="mix_in",
            side=(ffn2_w_in[l], ffn2_w_out[l], ssm_glu_w[l], conv_w_out[l], mix_w_out[l]))

        ys = _ssm(proj, ssm_a_re[l], ssm_a_im[l], ssm_log_dt[l], ssm_b_re[l], ssm_b_im[l],
                  ssm_c_re[l], ssm_c_im[l], ssm_d[l], n_chunks=n_chunks, name="ssm")

        cz = _conv(proj, conv_w[l], batch=batch, seq=seq, d_conv=d_conv, col0=d_ssm, name="conv")

        (merged,), (wq_b, wk_b, wv_b, wo_b) = _merge(
            ys, cz, glu_w_b, conv_w_out_b, proj, d_model=d_model, gate_col0=d_ssm + 3 * d_conv, name="merge",
            side=(xattn_wq[l], xattn_wk[l], xattn_wv[l], xattn_wo[l]))
        h, h_b = _mm_res(merged, mix_w_out_b, h, name="mix_out", bf16_copy=True)

        (q,), _ = _norm_mm(h_b, xattn_norm[l], wq_b, name="xattn_q", bm=1024)
        (k,), _ = _norm_mm(memf, mem_norm[l], wk_b, name="xattn_k")
        (v,), _ = _norm_mm(memf, mem_norm[l], wv_b, name="xattn_v")
        o = _attn(q, k, v, batch=batch, seq=seq, n_mem=n_mem, n_heads=n_heads, name="xattn")
        (h,) = _mm_res(o, wo_b, h, name="xattn_o")

        (h,), _ = _ffn(h, ffn2_norm[l], ffn2_w_in_b, ffn2_w_out_b, final_norm,
                       mode="final_norm" if last else "plain", name="ffn2")
    if depth == 0:
        raise NotImplementedError("depth 0")
    return h.reshape(batch, seq, d_model)
```

```python
import functools

import jax
import jax.numpy as jnp
from jax import lax
from jax.experimental import pallas as pl
from jax.experimental.pallas import tpu as pltpu

F32 = jnp.float32
BF16 = jnp.bfloat16
RMS_EPS = 1e-6

V7X_LANES = 128
V7X_MXU_DIM = 256
V7X_VMEM_BYTES = 64 * 1024 * 1024
VMEM_LIMIT = V7X_VMEM_BYTES - 4 * 1024 * 1024

SSM_CHUNK = 16
SSM_GROUPS_PER_BLOCK = 16
SSM_SUB = 4


def _params(dims):
    return pltpu.CompilerParams(dimension_semantics=dims, vmem_limit_bytes=VMEM_LIMIT)


def _blk(dim, pref):
    b = min(dim, pref)
    assert dim % b == 0, (dim, pref)
    return b


def _rms_rows(x, g):
    ms = jnp.mean(x * x, axis=-1, keepdims=True)
    return x * lax.rsqrt(ms + RMS_EPS) * g


def _row_chunks(n_rows, chunk, body):
    chunk = min(chunk, n_rows)
    assert n_rows % chunk == 0

    def step(c, carry):
        body(pl.ds(pl.multiple_of(c * chunk, chunk), chunk))
        return carry

    lax.fori_loop(0, n_rows // chunk, step, 0)


BF16_SUBLANES = 16


def _side_spec(shape, grid):
    r, c = shape
    n_steps = grid[0] * grid[1]
    best = None
    for rb in range(BF16_SUBLANES, r + 1, BF16_SUBLANES):
        if r % rb:
            continue
        for cb in range(V7X_LANES, c + 1, V7X_LANES):
            if c % cb == 0 and (r // rb) * (c // cb) <= n_steps and (best is None or rb * cb < best[0] * best[1]):
                best = (rb, cb)
    assert best is not None, (shape, grid)
    rb, cb = best
    ncb = c // cb
    last = (r // rb) * ncb - 1

    def index(i, j):
        blk = jnp.minimum(i * grid[1] + j, last)
        return blk // ncb, blk % ncb

    return pl.BlockSpec((rb, cb), index)


def _call(body, *, grid, in_specs, out_specs, out_shapes, scratch_shapes=(), args, side=(), name):
    n_in, n_out, n_side = len(in_specs), len(out_specs), len(side)

    def kernel(*refs):
        side_in = refs[n_in:n_in + n_side]
        outs = refs[n_in + n_side:n_in + n_side + n_out]
        side_out = refs[n_in + n_side + n_out:n_in + 2 * n_side + n_out]

        def side_cast():
            for src, dst in zip(side_in, side_out):
                dst[...] = src[...].astype(BF16)

        body(*refs[:n_in], *outs, *refs[n_in + 2 * n_side + n_out:], side_cast=side_cast)

    side_specs = [_side_spec(w.shape, grid) for w in side]
    res = pl.pallas_call(
        kernel,
        grid=grid,
        in_specs=list(in_specs) + side_specs,
        out_specs=list(out_specs) + side_specs,
        out_shape=list(out_shapes) + [jax.ShapeDtypeStruct(w.shape, BF16) for w in side],
        scratch_shapes=list(scratch_shapes),
        compiler_params=_params(("arbitrary",) * len(grid)),
        name=name,
    )(*args, *side)
    return tuple(res[:n_out]), tuple(res[n_out:])


def _ffn_kernel(x_ref, g_ref, wa_ref, wb_ref, wout_ref, g2_ref, *rest, mode, side_cast):
    if mode == "next_norm":
        o_ref, hn_ref, xn_ref = rest
    else:
        o_ref, xn_ref = rest
    f = pl.program_id(1)
    bm = x_ref.shape[0]

    @pl.when(f == 0)
    def _():
        def body(rows):
            xn_ref[rows, :] = _rms_rows(x_ref[rows, :], g_ref[...]).astype(BF16)
            o_ref[rows, :] = jnp.zeros((rows.size, o_ref.shape[1]), F32)

        _row_chunks(bm, 64, body)

    xn = xn_ref[...]
    a = jnp.dot(xn, wa_ref[...], preferred_element_type=F32)
    b = jnp.dot(xn, wb_ref[...], preferred_element_type=F32)
    gated = (a * jax.nn.sigmoid(a) * b).astype(BF16)
    side_cast()
    d = o_ref.shape[1]
    bn = min(d, 1024)
    for n0 in range(0, d, bn):
        o_ref[:, n0:n0 + bn] += jnp.dot(gated, wout_ref[:, n0:n0 + bn], preferred_element_type=F32)

    @pl.when(f == pl.num_programs(1) - 1)
    def _():
        def body(rows):
            h = x_ref[rows, :] + 0.5 * o_ref[rows, :]
            if mode == "final_norm":
                h = _rms_rows(h, g2_ref[...])
            elif mode == "next_norm":
                hn_ref[rows, :] = _rms_rows(h, g2_ref[...]).astype(BF16)
            o_ref[rows, :] = h

        _row_chunks(bm, 64, body)


def _ffn(x, g, w_in, w_out, g2, *, mode, name, side=()):
    t, d = x.shape
    f_dim = w_out.shape[0]
    bf = _blk(f_dim, V7X_MXU_DIM)
    nf = f_dim // bf
    bm = _blk(t, 512)
    w_in = w_in.astype(BF16)
    row_spec = pl.BlockSpec((bm, d), lambda i, f: (i, 0))
    out_specs, out_shapes = [row_spec], [jax.ShapeDtypeStruct((t, d), F32)]
    if mode == "next_norm":
        out_specs, out_shapes = out_specs + [row_spec], out_shapes + [jax.ShapeDtypeStruct((t, d), BF16)]
    return _call(
        functools.partial(_ffn_kernel, mode=mode),
        grid=(t // bm, nf),
        in_specs=[
            row_spec,
            pl.BlockSpec((1, d), lambda i, f: (0, 0)),
            pl.BlockSpec((d, bf), lambda i, f: (0, f)),
            pl.BlockSpec((d, bf), lambda i, f: (0, nf + f)),
            pl.BlockSpec((bf, d), lambda i, f: (f, 0)),
            pl.BlockSpec((1, d), lambda i, f: (0, 0)),
        ],
        out_specs=out_specs,
        out_shapes=out_shapes,
        scratch_shapes=[pltpu.VMEM((bm, d), BF16)],
        args=(x, g.reshape(1, d), w_in, w_in, w_out.astype(BF16), g2.reshape(1, d)),
        side=side,
        name=name,
    )


def _mm_kernel(a_ref, w_ref, o_ref, *, side_cast):
    o_ref[...] = jnp.dot(a_ref[...], w_ref[...], preferred_element_type=F32).astype(o_ref.dtype)
    side_cast()


def _mm(a, w, *, name, bm=1024, bn=1024, side=()):
    m, k = a.shape
    n = w.shape[1]
    bm = _blk(m, bm)
    bn = _blk(n, bn)
    return _call(
        _mm_kernel,
        grid=(m // bm, n // bn),
        in_specs=[
            pl.BlockSpec((bm, k), lambda i, j: (i, 0)),
            pl.BlockSpec((k, bn), lambda i, j: (0, j)),
        ],
        out_specs=[pl.BlockSpec((bm, bn), lambda i, j: (i, j))],
        out_shapes=[jax.ShapeDtypeStruct((m, n), BF16)],
        args=(a, w.astype(BF16)),
        side=side,
        name=name,
    )


def _norm_mm_kernel(a_ref, g_ref, *rest, n_w):
    w_refs, o_refs, an_ref = rest[:n_w], rest[n_w:2 * n_w], rest[2 * n_w]

    @pl.when(pl.program_id(1) == 0)
    def _():
        def body(rows):
            an_ref[rows, :] = _rms_rows(a_ref[rows, :].astype(F32), g_ref[...]).astype(BF16)

        _row_chunks(a_ref.shape[0], 64, body)

    for w_ref, o_ref in zip(w_refs, o_refs):
        o_ref[...] = jnp.dot(an_ref[...], w_ref[...], preferred_element_type=F32).astype(o_ref.dtype)


def _norm_mm(a, g, ws, *, name, bm=512, bn=1024):
    m, k = a.shape
    n = ws[0].shape[1]
    bm = _blk(m, bm)
    bn = _blk(n, bn)
    tile = pl.BlockSpec((bm, bn), lambda i, j: (i, j))
    return pl.pallas_call(
        functools.partial(_norm_mm_kernel, n_w=len(ws)),
        grid=(m // bm, n // bn),
        in_specs=[
            pl.BlockSpec((bm, k), lambda i, j: (i, 0)),
            pl.BlockSpec((1, k), lambda i, j: (0, 0)),
        ] + [pl.BlockSpec((k, bn), lambda i, j: (0, j))] * len(ws),
        out_specs=[tile] * len(ws),
        out_shape=[jax.ShapeDtypeStruct((m, n), BF16)] * len(ws),
        scratch_shapes=[pltpu.VMEM((bm, k), BF16)],
        compiler_params=_params(("arbitrary", "arbitrary")),
        name=name,
    )(a, g.reshape(1, k), *[w.astype(BF16) for w in ws])


def _mm_res_kernel(a_ref, w_ref, res_ref, o_ref, *copy_ref):
    bn = o_ref.shape[1]
    cn = min(bn, 2 * V7X_MXU_DIM)
    for n0 in range(0, bn, cn):
        cols = slice(n0, n0 + cn)
        out = res_ref[:, cols] + jnp.dot(a_ref[...], w_ref[:, cols], preferred_element_type=F32)
        o_ref[:, cols] = out
        for ref in copy_ref:
            ref[:, cols] = out.astype(ref.dtype)


def _mm_res(a, w, res, *, name, bm=1024, bn=1024, bf16_copy=False):
    m, k = a.shape
    n = w.shape[1]
    bm = _blk(m, bm)
    bn = _blk(n, bn)
    tile = pl.BlockSpec((bm, bn), lambda i, j: (i, j))
    return pl.pallas_call(
        _mm_res_kernel,
        grid=(m // bm, n // bn),
        in_specs=[
            pl.BlockSpec((bm, k), lambda i, j: (i, 0)),
            pl.BlockSpec((k, bn), lambda i, j: (0, j)),
            tile,
        ],
        out_specs=[tile] + [tile] * bf16_copy,
        out_shape=[jax.ShapeDtypeStruct((m, n), F32)] + [jax.ShapeDtypeStruct((m, n), BF16)] * bf16_copy,
        compiler_params=_params(("parallel", "arbitrary")),
        name=name,
    )(a, w.astype(BF16), res)


def _cmul(ar, ai, br, bi):
    return ar * br - ai * bi, ar * bi + ai * br


def _gelu_exact(x):
    return 0.5 * x * (1.0 + lax.erf(x * (2.0 ** -0.5)))


def _ssm_kernel(tok_ref, are_ref, aim_ref, ldt_ref, braw_ref, craw_ref, dsk_ref, o_ref,
                bstack_ref, nst_ref, dstack_ref, pw_ref, apow_ref, sre_ref, sim_ref, w_ref,
                slab_ref, u_ref, *, n_seq, n_chunks, n_log):
    w256 = V7X_MXU_DIM
    ns = braw_ref.shape[2] // 2
    n_strip = ns // V7X_LANES
    off = sre_ref.shape[2] - n_chunks
    rows = n_seq * n_chunks
    n_slab = w256 // V7X_LANES

    for hf in range(n_slab):
        slab_ref[hf] = tok_ref[:, hf * V7X_LANES:(hf + 1) * V7X_LANES].astype(F32)
    for r in range(SSM_CHUNK):
        parts = [slab_ref[hf, pl.ds(r, rows, stride=SSM_CHUNK), :] for hf in range(n_slab)]
        u_ref[r] = jnp.concatenate(parts, axis=1).astype(BF16)

    @pl.when(pl.program_id(1) == 0)
    def _prepare():
        ar = are_ref[0]
        ai = aim_ref[0]
        dt = jnp.exp(ldt_ref[0])
        mag = jnp.exp(ar * dt)
        lr = mag * jnp.cos(ai * dt)
        li = mag * jnp.sin(ai * dt)
        den = ar * ar + ai * ai
        zr = ((lr - 1.0) * ar + li * ai) / den
        zi = (li * ar - (lr - 1.0) * ai) / den
        sr, si = _cmul(braw_ref[0, :, :ns], braw_ref[0, :, ns:], zr, zi)
        cr = craw_ref[0, :, :ns]
        ci = craw_ref[0, :, ns:]
        cn = jnp.concatenate([cr, -ci], axis=1).astype(BF16)
        for j in range(SSM_CHUNK):
            bs = jnp.concatenate([sr, si], axis=1).astype(BF16)
            dj = lax.dot_general(bs, cn, (((1,), (1,)), ((), ())), preferred_element_type=F32)
            dstack_ref[(SSM_CHUNK - 1 - j) * w256:(SSM_CHUNK - j) * w256, :] = dj.astype(BF16)
            if j < SSM_SUB:
                bstack_ref[(SSM_SUB - 1 - j) * w256:(SSM_SUB - j) * w256, :] = bs
            sr, si = _cmul(sr, si, lr, li)
        mr, mi = lr, li
        for s in range(SSM_SUB):
            er, ei = _cmul(cr, ci, mr, mi)
            nst_ref[s * w256:(s + 1) * w256, :] = jnp.concatenate([er, -ei], axis=1).astype(BF16)
            if s < SSM_SUB - 1:
                mr, mi = _cmul(mr, mi, lr, li)
        pw_ref[0:1, :] = mr
        pw_ref[1:2, :] = mi
        p8r, p8i = _cmul(mr, mi, mr, mi)
        kr, ki = _cmul(p8r, p8i, p8r, p8i)
        for k in range(n_log):
            for s in range(n_strip):
                apow_ref[k, s, 0:1, :] = kr[:, s * V7X_LANES:(s + 1) * V7X_LANES]
                apow_ref[k, s, 1:2, :] = ki[:, s * V7X_LANES:(s + 1) * V7X_LANES]
            kr, ki = _cmul(kr, ki, kr, ki)
        zeros = jnp.zeros((off, V7X_LANES), F32)
        for s in range(n_strip):
            for q in range(n_seq):
                sre_ref[s, q, 0:off, :] = zeros
                sim_ref[s, q, 0:off, :] = zeros

    p4r = pw_ref[0:1, :]
    p4i = pw_ref[1:2, :]

    xr = xi = None
    for q in range(SSM_SUB):
        z = None
        for s in range(SSM_SUB):
            d = jnp.dot(u_ref[SSM_SUB * q + s], bstack_ref[s * w256:(s + 1) * w256, :],
                        preferred_element_type=F32)
            z = d if z is None else z + d
        if xr is None:
            xr, xi = z[:, :ns], z[:, ns:]
        else:
            xr, xi = _cmul(xr, xi, p4r, p4i)
            xr, xi = xr + z[:, :ns], xi + z[:, ns:]

    for s in range(n_strip):
        lanes = slice(s * V7X_LANES, (s + 1) * V7X_LANES)
        for q in range(n_seq):
            sre_ref[s, q, off:, :] = xr[q * n_chunks:(q + 1) * n_chunks, lanes]
            sim_ref[s, q, off:, :] = xi[q * n_chunks:(q + 1) * n_chunks, lanes]
    for s in range(n_strip):
        for q in range(n_seq):
            for k in range(n_log):
                sh = 1 << k
                kr = apow_ref[k, s, 0:1, :]
                ki = apow_ref[k, s, 1:2, :]
                pr = sre_ref[s, q, off - sh:off - sh + n_chunks, :]
                pi = sim_ref[s, q, off - sh:off - sh + n_chunks, :]
                tr, ti = _cmul(pr, pi, kr, ki)
                sre_ref[s, q, off:, :] = sre_ref[s, q, off:, :] + tr
                sim_ref[s, q, off:, :] = sim_ref[s, q, off:, :] + ti

    ysts = []
    for q in range(SSM_SUB):
        for s in range(n_strip):
            lanes_r = slice(s * V7X_LANES, (s + 1) * V7X_LANES)
            lanes_i = slice(ns + s * V7X_LANES, ns + (s + 1) * V7X_LANES)
            for sq in range(n_seq):
                rs = slice(sq * n_chunks, (sq + 1) * n_chunks)
                if q == 0:
                    vr = sre_ref[s, sq, off - 1:off - 1 + n_chunks, :]
                    vi = sim_ref[s, sq, off - 1:off - 1 + n_chunks, :]
                else:
                    vr, vi = _cmul(sre_ref[s, sq, off:, :], sim_ref[s, sq, off:, :],
                                   p4r[:, lanes_r], p4i[:, lanes_r])
                if q < SSM_SUB - 1:
                    sre_ref[s, sq, off:, :] = vr
                    sim_ref[s, sq, off:, :] = vi
                w_ref[rs, lanes_r] = vr.astype(BF16)
                w_ref[rs, lanes_i] = vi.astype(BF16)
        ysts.append(lax.dot_general(w_ref[...], nst_ref[...], (((1,), (1,)), ((), ())),
                                    preferred_element_type=F32))

    dsk = dsk_ref[0]
    for t in range(SSM_CHUNK):
        y = ysts[t // SSM_SUB][:, (t % SSM_SUB) * w256:(t % SSM_SUB + 1) * w256]
        y = y + dsk * u_ref[t].astype(F32)
        for r in range(t + 1):
            blk = SSM_CHUNK - 1 - t + r
            y = y + jnp.dot(u_ref[r], dstack_ref[blk * w256:(blk + 1) * w256, :],
                            preferred_element_type=F32)
        y = _gelu_exact(y)
        for hf in range(n_slab):
            slab_ref[hf, pl.ds(t, rows, stride=SSM_CHUNK), :] = y[:, hf * V7X_LANES:(hf + 1) * V7X_LANES]
    for hf in range(n_slab):
        o_ref[:, hf * V7X_LANES:(hf + 1) * V7X_LANES] = slab_ref[hf].astype(o_ref.dtype)


def _ssm(proj, a_re, a_im, log_dt, b_re, b_im, c_re, c_im, d_skip, *, n_chunks, name):
    n_tok = proj.shape[0]
    d_ssm = d_skip.shape[0]
    n_rows = n_tok // SSM_CHUNK
    n_groups, n_state = a_re.shape
    h = d_ssm // n_groups
    gpb = SSM_GROUPS_PER_BLOCK
    assert gpb * h == V7X_MXU_DIM and n_groups % gpb == 0
    n_blocks = n_groups // gpb
    ns = gpb * n_state
    n_seq = 1
    n_log = max(1, (n_chunks - 1).bit_length())
    off = max(8, 1 << (n_log - 1))

    col_group = (jnp.arange(2 * ns) % ns) // n_state
    same_group = (jnp.arange(gpb)[:, None] == col_group[None, :]).astype(F32)

    def block_diag(x_re, x_im):
        lanes = lambda x: x.reshape(n_blocks, gpb, h, n_state).transpose(0, 2, 1, 3).reshape(n_blocks, h, ns)
        x = jnp.concatenate([lanes(x_re), lanes(x_im)], axis=2)
        x = x[:, None, :, :] * same_group[None, :, None, :]
        return x.reshape(n_blocks, gpb * h, 2 * ns)

    braw = block_diag(b_re.transpose(0, 2, 1), b_im.transpose(0, 2, 1))
    craw = block_diag(c_re, c_im)
    vec = lambda x: x.reshape(n_blocks, 1, ns)
    ldt = jnp.broadcast_to(log_dt[:, None], (n_groups, n_state))
    rows = n_seq * n_chunks
    kernel = functools.partial(_ssm_kernel, n_seq=n_seq, n_chunks=n_chunks, n_log=n_log)
    vspec = pl.BlockSpec((1, 1, ns), lambda g, r: (g, 0, 0))
    mspec = pl.BlockSpec((1, V7X_MXU_DIM, 2 * ns), lambda g, r: (g, 0, 0))
    uspec = pl.BlockSpec((rows * SSM_CHUNK, V7X_MXU_DIM), lambda g, r: (r, g))
    return pl.pallas_call(
        kernel,
        grid=(n_blocks, n_rows // rows),
        in_specs=[uspec, vspec, vspec, vspec, mspec, mspec,
                  pl.BlockSpec((1, 1, V7X_MXU_DIM), lambda g, r: (g, 0, 0))],
        out_specs=uspec,
        out_shape=jax.ShapeDtypeStruct((n_tok, d_ssm), BF16),
        scratch_shapes=[
            pltpu.VMEM((SSM_SUB * V7X_MXU_DIM, 2 * ns), BF16),
            pltpu.VMEM((SSM_SUB * V7X_MXU_DIM, 2 * ns), BF16),
            pltpu.VMEM((SSM_CHUNK * V7X_MXU_DIM, V7X_MXU_DIM), BF16),
            pltpu.VMEM((8, ns), F32),
            pltpu.VMEM((n_log, ns // V7X_LANES, 8, V7X_LANES), F32),
            pltpu.VMEM((ns // V7X_LANES, n_seq, off + n_chunks, V7X_LANES), F32),
            pltpu.VMEM((ns // V7X_LANES, n_seq, off + n_chunks, V7X_LANES), F32),
            pltpu.VMEM((rows, 2 * ns), BF16),
            pltpu.VMEM((V7X_MXU_DIM // V7X_LANES, rows * SSM_CHUNK, V7X_LANES), F32),
            pltpu.VMEM((SSM_CHUNK, rows, V7X_MXU_DIM), BF16),
        ],
        compiler_params=_params(("parallel", "arbitrary")),
        name=name,
    )(proj, vec(a_re), vec(a_im), vec(ldt), braw, craw, d_skip.reshape(n_blocks, 1, V7X_MXU_DIM))


def _conv_kernel(cb_ref, cc_ref, ch_ref, w_ref, o_ref, z_ref, *, chunk):
    seq = cb_ref.shape[0]
    pad = z_ref.shape[0] - seq
    kw = w_ref.shape[0]
    z_ref[0:pad, :] = jnp.zeros((pad, z_ref.shape[1]), F32)

    def fill(rows):
        z_ref[pl.ds(rows.start + pad, rows.size), :] = cc_ref[rows, :].astype(F32) * ch_ref[rows, :].astype(F32)

    _row_chunks(seq, chunk, fill)
    for c in range(seq // chunk):
        r0 = c * chunk
        acc = None
        for k in range(kw):
            lag = kw - 1 - k
            term = w_ref[k:k + 1, :] * z_ref[pad + r0 - lag:pad + r0 - lag + chunk, :]
            acc = term if acc is None else acc + term
        o_ref[r0:r0 + chunk, :] = (cb_ref[r0:r0 + chunk, :].astype(F32) * acc).astype(o_ref.dtype)


def _conv(proj, conv_w, *, batch, seq, d_conv, col0, name):
    bc = _blk(d_conv, V7X_MXU_DIM)
    nb = d_conv // bc
    chunk = min(seq, 512)
    kw = conv_w.shape[0]
    cspec = lambda k: pl.BlockSpec((seq, bc), lambda b, j: (b, (col0 + k * d_conv) // bc + j))
    return pl.pallas_call(
        functools.partial(_conv_kernel, chunk=chunk),
        grid=(batch, nb),
        in_specs=[cspec(0), cspec(1), cspec(2), pl.BlockSpec((kw, bc), lambda b, j: (0, j))],
        out_specs=pl.BlockSpec((seq, bc), lambda b, j: (b, j)),
        out_shape=jax.ShapeDtypeStruct((batch * seq, d_conv), BF16),
        scratch_shapes=[pltpu.VMEM((8 + seq, bc), F32)],
        compiler_params=_params(("parallel", "parallel")),
        name=name,
    )(proj, proj, proj, conv_w)


def _merge_kernel(ys_ref, cz_ref, wv_ref, wg_ref, wc_ref, ga_ref, gb_ref, o_ref, *, side_cast):
    side_cast()
    ys = ys_ref[...]
    val = jnp.dot(ys, wv_ref[...], preferred_element_type=F32)
    gate = jnp.dot(ys, wg_ref[...], preferred_element_type=F32)
    yb = jnp.dot(cz_ref[...], wc_ref[...], preferred_element_type=F32)
    ya = val * jax.nn.sigmoid(gate)
    out = jax.nn.sigmoid(ga_ref[...].astype(F32)) * ya + jax.nn.sigmoid(gb_ref[...].astype(F32)) * yb
    o_ref[...] = out.astype(o_ref.dtype)


def _merge(ys, cz, glu_w, conv_w_out, proj, *, d_model, gate_col0, name, bm=1024, bn=512, side=()):
    t, k = ys.shape
    bm = _blk(t, bm)
    bn = _blk(d_model, bn)
    nj = d_model // bn
    aspec = pl.BlockSpec((bm, k), lambda i, j: (i, 0))
    wspec = lambda off: pl.BlockSpec((k, bn), lambda i, j: (0, off + j))
    gspec = lambda off: pl.BlockSpec((bm, bn), lambda i, j: (i, off + j))
    glu_w = glu_w.astype(BF16)
    return _call(
        _merge_kernel,
        grid=(t // bm, nj),
        in_specs=[aspec, aspec, wspec(0), wspec(nj), wspec(0),
                  gspec(gate_col0 // bn), gspec(gate_col0 // bn + nj)],
        out_specs=[pl.BlockSpec((bm, bn), lambda i, j: (i, j))],
        out_shapes=[jax.ShapeDtypeStruct((t, d_model), BF16)],
        args=(ys, cz, glu_w, glu_w, conv_w_out.astype(BF16), proj, proj),
        side=side,
        name=name,
    )


def _attend(q, k, v, scale):
    s = lax.dot_general(q.astype(BF16), k, (((1,), (1,)), ((), ())), preferred_element_type=F32) * scale
    p = jnp.exp(s - jnp.max(s, axis=-1, keepdims=True))
    denom = jnp.sum(p, axis=-1, keepdims=True)
    return jnp.dot(p.astype(BF16), v, preferred_element_type=F32) / denom


def _q_attn_kernel(a_ref, g_ref, wq_ref, k_ref, v_ref, o_ref, an_ref, *, scale, row_chains):
    @pl.when(pl.program_id(1) == 0)
    def _():
        def body(rows):
            an_ref[rows, :] = _rms_rows(a_ref[rows, :].astype(F32), g_ref[...]).astype(BF16)

        _row_chunks(a_ref.shape[0], 64, body)

    rc = a_ref.shape[0] // row_chains
    for r0 in range(0, a_ref.shape[0], rc):
        q = jnp.dot(an_ref[r0:r0 + rc, :], wq_ref[...], preferred_element_type=F32)
        o_ref[r0:r0 + rc, :] = _attend(q, k_ref[...], v_ref[...], scale).astype(o_ref.dtype)


def _q_attn(a, g, wq, k, v, *, batch, seq, n_mem, n_heads, name, bm=1024, row_chains=2):
    t, d = a.shape
    dh = d // n_heads
    bm = _blk(seq, bm)
    per_seq = seq // bm
    return pl.pallas_call(
        functools.partial(_q_attn_kernel, scale=dh ** -0.5, row_chains=row_chains),
        grid=(t // bm, n_heads),
        in_specs=[
            pl.BlockSpec((bm, d), lambda i, h: (i, 0)),
            pl.BlockSpec((1, d), lambda i, h: (0, 0)),
            pl.BlockSpec((d, dh), lambda i, h: (0, h)),
            pl.BlockSpec((n_mem, dh), lambda i, h: (i // per_seq, h)),
            pl.BlockSpec((n_mem, dh), lambda i, h: (i // per_seq, h)),
        ],
        out_specs=pl.BlockSpec((bm, dh), lambda i, h: (i, h)),
        out_shape=jax.ShapeDtypeStruct((t, d), BF16),
        scratch_shapes=[pltpu.VMEM((bm, d), BF16)],
        compiler_params=_params(("arbitrary", "arbitrary")),
        name=name,
    )(a, g.reshape(1, d), wq.astype(BF16), k, v)


def kernel(x, mem, ffn1_norm, ffn1_w_in, ffn1_w_out, mix_norm, mix_w_in, ssm_a_re, ssm_a_im, ssm_log_dt, ssm_b_re, ssm_b_im, ssm_c_re, ssm_c_im, ssm_d, ssm_glu_w, conv_w, conv_w_out, mix_w_out, xattn_norm, mem_norm, xattn_wq, xattn_wk, xattn_wv, xattn_wo, ffn2_norm, ffn2_w_in, ffn2_w_out, final_norm):
    batch, seq, d_model = x.shape
    n_mem = mem.shape[1]
    depth = ffn1_norm.shape[0]
    d_ssm = ssm_d.shape[1]
    d_conv = conv_w.shape[2]
    n_heads = 4
    t = batch * seq
    assert seq % SSM_CHUNK == 0
    n_chunks = seq // SSM_CHUNK

    h = x.reshape(t, d_model)
    memf = mem.reshape(batch * n_mem, d_model)
    for l in range(depth):
        last = l == depth - 1
        (h, un), (mix_w_in_b,) = _ffn(h, ffn1_norm[l], ffn1_w_in[l], ffn1_w_out[l], mix_norm[l], mode="next_norm",
                                      name="ffn1", side=(mix_w_in[l],))

        (proj,), (ffn2_w_in_b, ffn2_w_out_b, glu_w_b, conv_w_out_b, mix_w_out_b) = _mm(
            un, mix_w_in_b, name="mix_in",
            side=(ffn2_w_in[l], ffn2_w_out[l], ssm_glu_w[l], conv_w_out[l], mix_w_out[l]))

        ys = _ssm(proj, ssm_a_re[l], ssm_a_im[l], ssm_log_dt[l], ssm_b_re[l], ssm_b_im[l],
                  ssm_c_re[l], ssm_c_im[l], ssm_d[l], n_chunks=n_chunks, name="ssm")

        cz = _conv(proj, conv_w[l], batch=batch, seq=seq, d_conv=d_conv, col0=d_ssm, name="conv")

        (merged,), (wq_b, wk_b, wv_b, wo_b) = _merge(
            ys, cz, glu_w_b, conv_w_out_b, proj, d_model=d_model, gate_col0=d_ssm + 3 * d_conv, name="merge",
            side=(xattn_wq[l], xattn_wk[l], xattn_wv[l], xattn_wo[l]))
        (h,) = _mm_res(merged, mix_w_out_b, h, name="mix_out")

        k, v = _norm_mm(memf, mem_norm[l], (wk_b, wv_b), name="xattn_kv", bn=512)
        o = _q_attn(h, xattn_norm[l], wq_b, k, v, batch=batch, seq=seq, n_mem=n_mem, n_heads=n_heads,
                    name="xattn", bm=512)
        (h,) = _mm_res(o, wo_b, h, name="xattn_o")

        (h,), _ = _ffn(h, ffn2_norm[l], ffn2_w_in_b, ffn2_w_out_b, final_norm,
                       mode="final_norm" if last else "plain", name="ffn2")
    if depth == 0:
        raise NotImplementedError("depth 0")
    return h.reshape(batch, seq, d_model)
```

```python
import functools

import jax
import jax.numpy as jnp
from jax import lax
from jax.experimental import pallas as pl
from jax.experimental.pallas import tpu as pltpu

F32 = jnp.float32
BF16 = jnp.bfloat16
RMS_EPS = 1e-6

V7X_LANES = 128
V7X_MXU_DIM = 256
V7X_VMEM_BYTES = 64 * 1024 * 1024
VMEM_LIMIT = V7X_VMEM_BYTES - 4 * 1024 * 1024

SSM_CHUNK = 16
SSM_GROUPS_PER_BLOCK = 16
SSM_SUB = 4


def _params(dims):
    return pltpu.CompilerParams(dimension_semantics=dims, vmem_limit_bytes=VMEM_LIMIT)


def _blk(dim, pref):
    b = min(dim, pref)
    assert dim % b == 0, (dim, pref)
    return b


def _rms_rows(x, g):
    ms = jnp.mean(x * x, axis=-1, keepdims=True)
    return x * lax.rsqrt(ms + RMS_EPS) * g


def _row_chunks(n_rows, chunk, body):
    chunk = min(chunk, n_rows)
    assert n_rows % chunk == 0

    def step(c, carry):
        body(pl.ds(pl.multiple_of(c * chunk, chunk), chunk))
        return carry

    lax.fori_loop(0, n_rows // chunk, step, 0)


BF16_SUBLANES = 16


def _side_spec(shape, grid):
    r, c = shape
    n_steps = grid[0] * grid[1]
    best = None
    for rb in range(BF16_SUBLANES, r + 1, BF16_SUBLANES):
        if r % rb:
            continue
        for cb in range(V7X_LANES, c + 1, V7X_LANES):
            if c % cb == 0 and (r // rb) * (c // cb) <= n_steps and (best is None or rb * cb < best[0] * best[1]):
                best = (rb, cb)
    assert best is not None, (shape, grid)
    rb, cb = best
    ncb = c // cb
    last = (r // rb) * ncb - 1

    def index(i, j):
        blk = jnp.minimum(i * grid[1] + j, last)
        return blk // ncb, blk % ncb

    return pl.BlockSpec((rb, cb), index)


def _call(body, *, grid, in_specs, out_specs, out_shapes, scratch_shapes=(), args, side=(), name):
    n_in, n_out, n_side = len(in_specs), len(out_specs), len(side)

    def kernel(*refs):
        side_in = refs[n_in:n_in + n_side]
        outs = refs[n_in + n_side:n_in + n_side + n_out]
        side_out = refs[n_in + n_side + n_out:n_in + 2 * n_side + n_out]

        def side_cast():
            for src, dst in zip(side_in, side_out):
                dst[...] = src[...].astype(BF16)

        body(*refs[:n_in], *outs, *refs[n_in + 2 * n_side + n_out:], side_cast=side_cast)

    side_specs = [_side_spec(w.shape, grid) for w in side]
    res = pl.pallas_call(
        kernel,
        grid=grid,
        in_specs=list(in_specs) + side_specs,
        out_specs=list(out_specs) + side_specs,
        out_shape=list(out_shapes) + [jax.ShapeDtypeStruct(w.shape, BF16) for w in side],
        scratch_shapes=list(scratch_shapes),
        compiler_params=_params(("arbitrary",) * len(grid)),
        name=name,
    )(*args, *side)
    return tuple(res[:n_out]), tuple(res[n_out:])


def _ffn_kernel(x_ref, g_ref, wa_ref, wb_ref, wout_ref, g2_ref, *rest, mode, side_cast):
    if mode == "next_norm":
        o_ref, hn_ref, xn_ref = rest
    else:
        o_ref, xn_ref = rest
    f = pl.program_id(1)
    bm = x_ref.shape[0]

    @pl.when(f == 0)
    def _():
        def body(rows):
            xn_ref[rows, :] = _rms_rows(x_ref[rows, :], g_ref[...]).astype(BF16)
            o_ref[rows, :] = jnp.zeros((rows.size, o_ref.shape[1]), F32)

        _row_chunks(bm, 64, body)

    xn = xn_ref[...]
    a = jnp.dot(xn, wa_ref[...], preferred_element_type=F32)
    b = jnp.dot(xn, wb_ref[...], preferred_element_type=F32)
    gated = (a * jax.nn.sigmoid(a) * b).astype(BF16)
    side_cast()
    d = o_ref.shape[1]
    bn = min(d, 1024)
    for n0 in range(0, d, bn):
        o_ref[:, n0:n0 + bn] += jnp.dot(gated, wout_ref[:, n0:n0 + bn], preferred_element_type=F32)

    @pl.when(f == pl.num_programs(1) - 1)
    def _():
        def body(rows):
            h = x_ref[rows, :] + 0.5 * o_ref[rows, :]
            if mode == "final_norm":
                h = _rms_rows(h, g2_ref[...])
            elif mode == "next_norm":
                hn_ref[rows, :] = _rms_rows(h, g2_ref[...]).astype(BF16)
            o_ref[rows, :] = h

        _row_chunks(bm, 64, body)


def _ffn(x, g, w_in, w_out, g2, *, mode, name, side=()):
    t, d = x.shape
    f_dim = w_out.shape[0]
    bf = _blk(f_dim, V7X_MXU_DIM)
    nf = f_dim // bf
    bm = _blk(t, 512)
    w_in = w_in.astype(BF16)
    row_spec = pl.BlockSpec((bm, d), lambda i, f: (i, 0))
    out_specs, out_shapes = [row_spec], [jax.ShapeDtypeStruct((t, d), F32)]
    if mode == "next_norm":
        out_specs, out_shapes = out_specs + [row_spec], out_shapes + [jax.ShapeDtypeStruct((t, d), BF16)]
    return _call(
        functools.partial(_ffn_kernel, mode=mode),
        grid=(t // bm, nf),
        in_specs=[
            row_spec,
            pl.BlockSpec((1, d), lambda i, f: (0, 0)),
            pl.BlockSpec((d, bf), lambda i, f: (0, f)),
            pl.BlockSpec((d, bf), lambda i, f: (0, nf + f)),
            pl.BlockSpec((bf, d), lambda i, f: (f, 0)),
            pl.BlockSpec((1, d), lambda i, f: (0, 0)),
        ],
        out_specs=out_specs,
        out_shapes=out_shapes,
        scratch_shapes=[pltpu.VMEM((bm, d), BF16)],
        args=(x, g.reshape(1, d), w_in, w_in, w_out.astype(BF16), g2.reshape(1, d)),
        side=side,
        name=name,
    )


def _ffn_loop_kernel(x_ref, g_ref, g2_ref, win_hbm, wout_hbm, o_ref, xn_ref, wa_buf, wb_buf, wo_buf, sem,
                     *, mode, nf, bf):
    bm, d = x_ref.shape

    def copies_in(f, slot):
        return (
            pltpu.make_async_copy(win_hbm.at[:, pl.ds(f * bf, bf)], wa_buf.at[slot], sem.at[0, slot]),
            pltpu.make_async_copy(win_hbm.at[:, pl.ds((nf + f) * bf, bf)], wb_buf.at[slot], sem.at[1, slot]),
        )

    def copies_out(f, slot):
        return (pltpu.make_async_copy(wout_hbm.at[pl.ds(f * bf, bf), :], wo_buf.at[pl.ds(slot * bf, bf), :],
                                      sem.at[2, slot]),)

    def start(copies):
        for c in copies:
            c.start()

    def wait(copies):
        for c in copies:
            c.wait()

    start(copies_in(0, 0))
    start(copies_out(0, 0))

    def prologue(rows):
        xn_ref[rows, :] = _rms_rows(x_ref[rows, :], g_ref[...]).astype(BF16)
        o_ref[rows, :] = jnp.zeros((rows.size, d), F32)

    _row_chunks(bm, 64, prologue)

    def gate(slot):
        xn = xn_ref[...]
        a = jnp.dot(xn, wa_buf[slot], preferred_element_type=F32)
        b = jnp.dot(xn, wb_buf[slot], preferred_element_type=F32)
        return (a * jax.nn.sigmoid(a) * b).astype(BF16)

    def project(gated, w_rows):
        bn = min(d, 1024)
        for n0 in range(0, d, bn):
            o_ref[:, n0:n0 + bn] += jnp.dot(gated, wo_buf[w_rows, n0:n0 + bn], preferred_element_type=F32)

    def two_tiles(f0, more):
        start(copies_in(f0 + 1, 1))
        start(copies_out(f0 + 1, 1))
        wait(copies_in(f0, 0))
        g0 = gate(0)
        if more:
            start(copies_in(f0 + 2, 0))
        wait(copies_in(f0 + 1, 1))
        g1 = gate(1)
        wait(copies_out(f0, 0))
        wait(copies_out(f0 + 1, 1))
        project(jnp.concatenate([g0, g1], axis=1), slice(0, 2 * bf))
        if more:
            start(copies_out(f0 + 2, 0))

    n_pairs = (nf - 1) // 2

    def pair(p, carry):
        two_tiles(2 * p, True)
        return carry

    lax.fori_loop(0, n_pairs, pair, 0)
    f_last = 2 * n_pairs
    if nf - f_last == 2:
        two_tiles(f_last, False)
    else:
        wait(copies_in(f_last, 0))
        g0 = gate(0)
        wait(copies_out(f_last, 0))
        project(g0, slice(0, bf))

    def epilogue(rows):
        h = x_ref[rows, :] + 0.5 * o_ref[rows, :]
        if mode == "final_norm":
            h = _rms_rows(h, g2_ref[...])
        o_ref[rows, :] = h

    _row_chunks(bm, 64, epilogue)


def _ffn_loop(x, g, w_in, w_out, g2, *, mode, name):
    t, d = x.shape
    f_dim = w_out.shape[0]
    bf = _blk(f_dim, V7X_MXU_DIM)
    nf = f_dim // bf
    bm = _blk(t, 512)
    row_spec = pl.BlockSpec((bm, d), lambda i: (i, 0))
    vec_spec = pl.BlockSpec((1, d), lambda i: (0, 0))
    return pl.pallas_call(
        functools.partial(_ffn_loop_kernel, mode=mode, nf=nf, bf=bf),
        grid=(t // bm,),
        in_specs=[row_spec, vec_spec, vec_spec,
                  pl.BlockSpec(memory_space=pl.ANY), pl.BlockSpec(memory_space=pl.ANY)],
        out_specs=row_spec,
        out_shape=jax.ShapeDtypeStruct((t, d), F32),
        scratch_shapes=[
            pltpu.VMEM((bm, d), BF16),
            pltpu.VMEM((2, d, bf), BF16),
            pltpu.VMEM((2, d, bf), BF16),
            pltpu.VMEM((2 * bf, d), BF16),
            pltpu.SemaphoreType.DMA((3, 2)),
        ],
        compiler_params=_params(("arbitrary",)),
        name=name,
    )(x, g.reshape(1, d), g2.reshape(1, d), w_in.astype(BF16), w_out.astype(BF16))


def _mm_kernel(a_ref, w_ref, o_ref, *, side_cast):
    o_ref[...] = jnp.dot(a_ref[...], w_ref[...], preferred_element_type=F32).astype(o_ref.dtype)
    side_cast()


def _mm(a, w, *, name, bm=1024, bn=1024, side=()):
    m, k = a.shape
    n = w.shape[1]
    bm = _blk(m, bm)
    bn = _blk(n, bn)
    return _call(
        _mm_kernel,
        grid=(m // bm, n // bn),
        in_specs=[
            pl.BlockSpec((bm, k), lambda i, j: (i, 0)),
            pl.BlockSpec((k, bn), lambda i, j: (0, j)),
        ],
        out_specs=[pl.BlockSpec((bm, bn), lambda i, j: (i, j))],
        out_shapes=[jax.ShapeDtypeStruct((m, n), BF16)],
        args=(a, w.astype(BF16)),
        side=side,
        name=name,
    )


def _norm_mm_kernel(a_ref, g_ref, *rest, n_w):
    w_refs, o_refs, an_ref = rest[:n_w], rest[n_w:2 * n_w], rest[2 * n_w]

    @pl.when(pl.program_id(1) == 0)
    def _():
        def body(rows):
            an_ref[rows, :] = _rms_rows(a_ref[rows, :].astype(F32), g_ref[...]).astype(BF16)

        _row_chunks(a_ref.shape[0], 64, body)

    for w_ref, o_ref in zip(w_refs, o_refs):
        o_ref[...] = jnp.dot(an_ref[...], w_ref[...], preferred_element_type=F32).astype(o_ref.dtype)


def _norm_mm(a, g, ws, *, name, bm=512, bn=1024):
    m, k = a.shape
    n = ws[0].shape[1]
    bm = _blk(m, bm)
    bn = _blk(n, bn)
    tile = pl.BlockSpec((bm, bn), lambda i, j: (i, j))
    return pl.pallas_call(
        functools.partial(_norm_mm_kernel, n_w=len(ws)),
        grid=(m // bm, n // bn),
        in_specs=[
            pl.BlockSpec((bm, k), lambda i, j: (i, 0)),
            pl.BlockSpec((1, k), lambda i, j: (0, 0)),
        ] + [pl.BlockSpec((k, bn), lambda i, j: (0, j))] * len(ws),
        out_specs=[tile] * len(ws),
        out_shape=[jax.ShapeDtypeStruct((m, n), BF16)] * len(ws),
        scratch_shapes=[pltpu.VMEM((bm, k), BF16)],
        compiler_params=_params(("arbitrary", "arbitrary")),
        name=name,
    )(a, g.reshape(1, k), *[w.astype(BF16) for w in ws])


def _mm_res_kernel(a_ref, w_ref, res_ref, o_ref, *copy_ref):
    bn = o_ref.shape[1]
    cn = min(bn, 2 * V7X_MXU_DIM)
    for n0 in range(0, bn, cn):
        cols = slice(n0, n0 + cn)
        out = res_ref[:, cols] + jnp.dot(a_ref[...], w_ref[:, cols], preferred_element_type=F32)
        o_ref[:, cols] = out
        for ref in copy_ref:
            ref[:, cols] = out.astype(ref.dtype)


def _mm_res(a, w, res, *, name, bm=1024, bn=1024, bf16_copy=False):
    m, k = a.shape
    n = w.shape[1]
    bm = _blk(m, bm)
    bn = _blk(n, bn)
    tile = pl.BlockSpec((bm, bn), lambda i, j: (i, j))
    return pl.pallas_call(
        _mm_res_kernel,
        grid=(m // bm, n // bn),
        in_specs=[
            pl.BlockSpec((bm, k), lambda i, j: (i, 0)),
            pl.BlockSpec((k, bn), lambda i, j: (0, j)),
            tile,
        ],
        out_specs=[tile] + [tile] * bf16_copy,
        out_shape=[jax.ShapeDtypeStruct((m, n), F32)] + [jax.ShapeDtypeStruct((m, n), BF16)] * bf16_copy,
        compiler_params=_params(("parallel", "arbitrary")),
        name=name,
    )(a, w.astype(BF16), res)


def _cmul(ar, ai, br, bi):
    return ar * br - ai * bi, ar * bi + ai * br


def _gelu_exact(x):
    return 0.5 * x * (1.0 + lax.erf(x * (2.0 ** -0.5)))


def _ssm_kernel(tok_ref, are_ref, aim_ref, ldt_ref, braw_ref, craw_ref, dsk_ref, o_ref,
                bstack_ref, nst_ref, dstack_ref, pw_ref, apow_ref, sre_ref, sim_ref, w_ref,
                slab_ref, u_ref, *, n_seq, n_chunks, n_log):
    w256 = V7X_MXU_DIM
    ns = braw_ref.shape[2] // 2
    n_strip = ns // V7X_LANES
    off = sre_ref.shape[2] - n_chunks
    rows = n_seq * n_chunks
    n_slab = w256 // V7X_LANES

    for hf in range(n_slab):
        slab_ref[hf] = tok_ref[:, hf * V7X_LANES:(hf + 1) * V7X_LANES].astype(F32)
    for r in range(SSM_CHUNK):
        parts = [slab_ref[hf, pl.ds(r, rows, stride=SSM_CHUNK), :] for hf in range(n_slab)]
        u_ref[r] = jnp.concatenate(parts, axis=1).astype(BF16)

    @pl.when(pl.program_id(1) == 0)
    def _prepare():
        ar = are_ref[0]
        ai = aim_ref[0]
        dt = jnp.exp(ldt_ref[0])
        mag = jnp.exp(ar * dt)
        lr = mag * jnp.cos(ai * dt)
        li = mag * jnp.sin(ai * dt)
        den = ar * ar + ai * ai
        zr = ((lr - 1.0) * ar + li * ai) / den
        zi = (li * ar - (lr - 1.0) * ai) / den
        sr, si = _cmul(braw_ref[0, :, :ns], braw_ref[0, :, ns:], zr, zi)
        cr = craw_ref[0, :, :ns]
        ci = craw_ref[0, :, ns:]
        cn = jnp.concatenate([cr, -ci], axis=1).astype(BF16)
        for j in range(SSM_CHUNK):
            bs = jnp.concatenate([sr, si], axis=1).astype(BF16)
            dj = lax.dot_general(bs, cn, (((1,), (1,)), ((), ())), preferred_element_type=F32)
            dstack_ref[(SSM_CHUNK - 1 - j) * w256:(SSM_CHUNK - j) * w256, :] = dj.astype(BF16)
            if j < SSM_SUB:
                bstack_ref[(SSM_SUB - 1 - j) * w256:(SSM_SUB - j) * w256, :] = bs
            sr, si = _cmul(sr, si, lr, li)
        mr, mi = lr, li
        for s in range(SSM_SUB):
            er, ei = _cmul(cr, ci, mr, mi)
            nst_ref[s * w256:(s + 1) * w256, :] = jnp.concatenate([er, -ei], axis=1).astype(BF16)
            if s < SSM_SUB - 1:
                mr, mi = _cmul(mr, mi, lr, li)
        pw_ref[0:1, :] = mr
        pw_ref[1:2, :] = mi
        p8r, p8i = _cmul(mr, mi, mr, mi)
        kr, ki = _cmul(p8r, p8i, p8r, p8i)
        for k in range(n_log):
            for s in range(n_strip):
                apow_ref[k, s, 0:1, :] = kr[:, s * V7X_LANES:(s + 1) * V7X_LANES]
                apow_ref[k, s, 1:2, :] = ki[:, s * V7X_LANES:(s + 1) * V7X_LANES]
            kr, ki = _cmul(kr, ki, kr, ki)
        zeros = jnp.zeros((off, V7X_LANES), F32)
        for s in range(n_strip):
            for q in range(n_seq):
                sre_ref[s, q, 0:off, :] = zeros
                sim_ref[s, q, 0:off, :] = zeros

    p4r = pw_ref[0:1, :]
    p4i = pw_ref[1:2, :]

    xr = xi = None
    for q in range(SSM_SUB):
        z = None
        for s in range(SSM_SUB):
            d = jnp.dot(u_ref[SSM_SUB * q + s], bstack_ref[s * w256:(s + 1) * w256, :],
                        preferred_element_type=F32)
            z = d if z is None else z + d
        if xr is None:
            xr, xi = z[:, :ns], z[:, ns:]
        else:
            xr, xi = _cmul(xr, xi, p4r, p4i)
            xr, xi = xr + z[:, :ns], xi + z[:, ns:]

    for s in range(n_strip):
        lanes = slice(s * V7X_LANES, (s + 1) * V7X_LANES)
        for q in range(n_seq):
            sre_ref[s, q, off:, :] = xr[q * n_chunks:(q + 1) * n_chunks, lanes]
            sim_ref[s, q, off:, :] = xi[q * n_chunks:(q + 1) * n_chunks, lanes]
    for s in range(n_strip):
        for q in range(n_seq):
            for k in range(n_log):
                sh = 1 << k
                kr = apow_ref[k, s, 0:1, :]
                ki = apow_ref[k, s, 1:2, :]
                pr = sre_ref[s, q, off - sh:off - sh + n_chunks, :]
                pi = sim_ref[s, q, off - sh:off - sh + n_chunks, :]
                tr, ti = _cmul(pr, pi, kr, ki)
                sre_ref[s, q, off:, :] = sre_ref[s, q, off:, :] + tr
                sim_ref[s, q, off:, :] = sim_ref[s, q, off:, :] + ti

    ysts = []
    for q in range(SSM_SUB):
        for s in range(n_strip):
            lanes_r = slice(s * V7X_LANES, (s + 1) * V7X_LANES)
            lanes_i = slice(ns + s * V7X_LANES, ns + (s + 1) * V7X_LANES)
            for sq in range(n_seq):
                rs = slice(sq * n_chunks, (sq + 1) * n_chunks)
                if q == 0:
                    vr = sre_ref[s, sq, off - 1:off - 1 + n_chunks, :]
                    vi = sim_ref[s, sq, off - 1:off - 1 + n_chunks, :]
                else:
                    vr, vi = _cmul(sre_ref[s, sq, off:, :], sim_ref[s, sq, off:, :],
                                   p4r[:, lanes_r], p4i[:, lanes_r])
                if q < SSM_SUB - 1:
                    sre_ref[s, sq, off:, :] = vr
                    sim_ref[s, sq, off:, :] = vi
                w_ref[rs, lanes_r] = vr.astype(BF16)
                w_ref[rs, lanes_i] = vi.astype(BF16)
        ysts.append(lax.dot_general(w_ref[...], nst_ref[...], (((1,), (1,)), ((), ())),
                                    preferred_element_type=F32))

    dsk = dsk_ref[0]
    for t in range(SSM_CHUNK):
        y = ysts[t // SSM_SUB][:, (t % SSM_SUB) * w256:(t % SSM_SUB + 1) * w256]
        y = y + dsk * u_ref[t].astype(F32)
        for r in range(t + 1):
            blk = SSM_CHUNK - 1 - t + r
            y = y + jnp.dot(u_ref[r], dstack_ref[blk * w256:(blk + 1) * w256, :],
                            preferred_element_type=F32)
        y = _gelu_exact(y)
        for hf in range(n_slab):
            slab_ref[hf, pl.ds(t, rows, stride=SSM_CHUNK), :] = y[:, hf * V7X_LANES:(hf + 1) * V7X_LANES]
    for hf in range(n_slab):
        o_ref[:, hf * V7X_LANES:(hf + 1) * V7X_LANES] = slab_ref[hf].astype(o_ref.dtype)


def _ssm(proj, a_re, a_im, log_dt, b_re, b_im, c_re, c_im, d_skip, *, n_chunks, name):
    n_tok = proj.shape[0]
    d_ssm = d_skip.shape[0]
    n_rows = n_tok // SSM_CHUNK
    n_groups, n_state = a_re.shape
    h = d_ssm // n_groups
    gpb = SSM_GROUPS_PER_BLOCK
    assert gpb * h == V7X_MXU_DIM and n_groups % gpb == 0
    n_blocks = n_groups // gpb
    ns = gpb * n_state
    n_seq = 1
    n_log = max(1, (n_chunks - 1).bit_length())
    off = max(8, 1 << (n_log - 1))

    col_group = (jnp.arange(2 * ns) % ns) // n_state
    same_group = (jnp.arange(gpb)[:, None] == col_group[None, :]).astype(F32)

    def block_diag(x_re, x_im):
        lanes = lambda x: x.reshape(n_blocks, gpb, h, n_state).transpose(0, 2, 1, 3).reshape(n_blocks, h, ns)
        x = jnp.concatenate([lanes(x_re), lanes(x_im)], axis=2)
        x = x[:, None, :, :] * same_group[None, :, None, :]
        return x.reshape(n_blocks, gpb * h, 2 * ns)

    braw = block_diag(b_re.transpose(0, 2, 1), b_im.transpose(0, 2, 1))
    craw = block_diag(c_re, c_im)
    vec = lambda x: x.reshape(n_blocks, 1, ns)
    ldt = jnp.broadcast_to(log_dt[:, None], (n_groups, n_state))
    rows = n_seq * n_chunks
    kernel = functools.partial(_ssm_kernel, n_seq=n_seq, n_chunks=n_chunks, n_log=n_log)
    vspec = pl.BlockSpec((1, 1, ns), lambda g, r: (g, 0, 0))
    mspec = pl.BlockSpec((1, V7X_MXU_DIM, 2 * ns), lambda g, r: (g, 0, 0))
    uspec = pl.BlockSpec((rows * SSM_CHUNK, V7X_MXU_DIM), lambda g, r: (r, g))
    return pl.pallas_call(
        kernel,
        grid=(n_blocks, n_rows // rows),
        in_specs=[uspec, vspec, vspec, vspec, mspec, mspec,
                  pl.BlockSpec((1, 1, V7X_MXU_DIM), lambda g, r: (g, 0, 0))],
        out_specs=uspec,
        out_shape=jax.ShapeDtypeStruct((n_tok, d_ssm), BF16),
        scratch_shapes=[
            pltpu.VMEM((SSM_SUB * V7X_MXU_DIM, 2 * ns), BF16),
            pltpu.VMEM((SSM_SUB * V7X_MXU_DIM, 2 * ns), BF16),
            pltpu.VMEM((SSM_CHUNK * V7X_MXU_DIM, V7X_MXU_DIM), BF16),
            pltpu.VMEM((8, ns), F32),
            pltpu.VMEM((n_log, ns // V7X_LANES, 8, V7X_LANES), F32),
            pltpu.VMEM((ns // V7X_LANES, n_seq, off + n_chunks, V7X_LANES), F32),
            pltpu.VMEM((ns // V7X_LANES, n_seq, off + n_chunks, V7X_LANES), F32),
            pltpu.VMEM((rows, 2 * ns), BF16),
            pltpu.VMEM((V7X_MXU_DIM // V7X_LANES, rows * SSM_CHUNK, V7X_LANES), F32),
            pltpu.VMEM((SSM_CHUNK, rows, V7X_MXU_DIM), BF16),
        ],
        compiler_params=_params(("parallel", "arbitrary")),
        name=name,
    )(proj, vec(a_re), vec(a_im), vec(ldt), braw, craw, d_skip.reshape(n_blocks, 1, V7X_MXU_DIM))


def _conv_kernel(cb_ref, cc_ref, ch_ref, w_ref, o_ref, z_ref, *, chunk):
    seq = cb_ref.shape[0]
    pad = z_ref.shape[0] - seq
    kw = w_ref.shape[0]
    z_ref[0:pad, :] = jnp.zeros((pad, z_ref.shape[1]), F32)

    def fill(rows):
        z_ref[pl.ds(rows.start + pad, rows.size), :] = cc_ref[rows, :].astype(F32) * ch_ref[rows, :].astype(F32)

    _row_chunks(seq, chunk, fill)
    for c in range(seq // chunk):
        r0 = c * chunk
        acc = None
        for k in range(kw):
            lag = kw - 1 - k
            term = w_ref[k:k + 1, :] * z_ref[pad + r0 - lag:pad + r0 - lag + chunk, :]
            acc = term if acc is None else acc + term
        o_ref[r0:r0 + chunk, :] = (cb_ref[r0:r0 + chunk, :].astype(F32) * acc).astype(o_ref.dtype)


def _conv(proj, conv_w, *, batch, seq, d_conv, col0, name):
    bc = _blk(d_conv, V7X_MXU_DIM)
    nb = d_conv // bc
    chunk = min(seq, 512)
    kw = conv_w.shape[0]
    cspec = lambda k: pl.BlockSpec((seq, bc), lambda b, j: (b, (col0 + k * d_conv) // bc + j))
    return pl.pallas_call(
        functools.partial(_conv_kernel, chunk=chunk),
        grid=(batch, nb),
        in_specs=[cspec(0), cspec(1), cspec(2), pl.BlockSpec((kw, bc), lambda b, j: (0, j))],
        out_specs=pl.BlockSpec((seq, bc), lambda b, j: (b, j)),
        out_shape=jax.ShapeDtypeStruct((batch * seq, d_conv), BF16),
        scratch_shapes=[pltpu.VMEM((8 + seq, bc), F32)],
        compiler_params=_params(("parallel", "parallel")),
        name=name,
    )(proj, proj, proj, conv_w)


def _merge_kernel(ys_ref, cz_ref, wv_ref, wg_ref, wc_ref, ga_ref, gb_ref, o_ref, *, side_cast):
    side_cast()
    ys = ys_ref[...]
    val = jnp.dot(ys, wv_ref[...], preferred_element_type=F32)
    gate = jnp.dot(ys, wg_ref[...], preferred_element_type=F32)
    yb = jnp.dot(cz_ref[...], wc_ref[...], preferred_element_type=F32)
    ya = val * jax.nn.sigmoid(gate)
    out = jax.nn.sigmoid(ga_ref[...].astype(F32)) * ya + jax.nn.sigmoid(gb_ref[...].astype(F32)) * yb
    o_ref[...] = out.astype(o_ref.dtype)


def _merge(ys, cz, glu_w, conv_w_out, proj, *, d_model, gate_col0, name, bm=1024, bn=512, side=()):
    t, k = ys.shape
    bm = _blk(t, bm)
    bn = _blk(d_model, bn)
    nj = d_model // bn
    aspec = pl.BlockSpec((bm, k), lambda i, j: (i, 0))
    wspec = lambda off: pl.BlockSpec((k, bn), lambda i, j: (0, off + j))
    gspec = lambda off: pl.BlockSpec((bm, bn), lambda i, j: (i, off + j))
    glu_w = glu_w.astype(BF16)
    return _call(
        _merge_kernel,
        grid=(t // bm, nj),
        in_specs=[aspec, aspec, wspec(0), wspec(nj), wspec(0),
                  gspec(gate_col0 // bn), gspec(gate_col0 // bn + nj)],
        out_specs=[pl.BlockSpec((bm, bn), lambda i, j: (i, j))],
        out_shapes=[jax.ShapeDtypeStruct((t, d_model), BF16)],
        args=(ys, cz, glu_w, glu_w, conv_w_out.astype(BF16), proj, proj),
        side=side,
        name=name,
    )


def _attend(q, k, v, scale):
    s = lax.dot_general(q.astype(BF16), k, (((1,), (1,)), ((), ())), preferred_element_type=F32) * scale
    p = jnp.exp(s - jnp.max(s, axis=-1, keepdims=True))
    denom = jnp.sum(p, axis=-1, keepdims=True)
    return jnp.dot(p.astype(BF16), v, preferred_element_type=F32) / denom


def _q_attn_kernel(a_ref, g_ref, wq_ref, k_ref, v_ref, o_ref, an_ref, *, scale, row_chains):
    @pl.when(pl.program_id(1) == 0)
    def _():
        def body(rows):
            an_ref[rows, :] = _rms_rows(a_ref[rows, :].astype(F32), g_ref[...]).astype(BF16)

        _row_chunks(a_ref.shape[0], 64, body)

    rc = a_ref.shape[0] // row_chains
    for r0 in range(0, a_ref.shape[0], rc):
        q = jnp.dot(an_ref[r0:r0 + rc, :], wq_ref[...], preferred_element_type=F32)
        o_ref[r0:r0 + rc, :] = _attend(q, k_ref[...], v_ref[...], scale).astype(o_ref.dtype)


def _q_attn(a, g, wq, k, v, *, batch, seq, n_mem, n_heads, name, bm=1024, row_chains=2):
    t, d = a.shape
    dh = d // n_heads
    bm = _blk(seq, bm)
    per_seq = seq // bm
    return pl.pallas_call(
        functools.partial(_q_attn_kernel, scale=dh ** -0.5, row_chains=row_chains),
        grid=(t // bm, n_heads),
        in_specs=[
            pl.BlockSpec((bm, d), lambda i, h: (i, 0)),
            pl.BlockSpec((1, d), lambda i, h: (0, 0)),
            pl.BlockSpec((d, dh), lambda i, h: (0, h)),
            pl.BlockSpec((n_mem, dh), lambda i, h: (i // per_seq, h)),
            pl.BlockSpec((n_mem, dh), lambda i, h: (i // per_seq, h)),
        ],
        out_specs=pl.BlockSpec((bm, dh), lambda i, h: (i, h)),
        out_shape=jax.ShapeDtypeStruct((t, d), BF16),
        scratch_shapes=[pltpu.VMEM((bm, d), BF16)],
        compiler_params=_params(("arbitrary", "arbitrary")),
        name=name,
    )(a, g.reshape(1, d), wq.astype(BF16), k, v)


def kernel(x, mem, ffn1_norm, ffn1_w_in, ffn1_w_out, mix_norm, mix_w_in, ssm_a_re, ssm_a_im, ssm_log_dt, ssm_b_re, ssm_b_im, ssm_c_re, ssm_c_im, ssm_d, ssm_glu_w, conv_w, conv_w_out, mix_w_out, xattn_norm, mem_norm, xattn_wq, xattn_wk, xattn_wv, xattn_wo, ffn2_norm, ffn2_w_in, ffn2_w_out, final_norm):
    batch, seq, d_model = x.shape
    n_mem = mem.shape[1]
    depth = ffn1_norm.shape[0]
    d_ssm = ssm_d.shape[1]
    d_conv = conv_w.shape[2]
    n_heads = 4
    t = batch * seq
    assert seq % SSM_CHUNK == 0
    n_chunks = seq // SSM_CHUNK

    h = x.reshape(t, d_model)
    memf = mem.reshape(batch * n_mem, d_model)
    for l in range(depth):
        last = l == depth - 1
        (h, un), (mix_w_in_b,) = _ffn(h, ffn1_norm[l], ffn1_w_in[l], ffn1_w_out[l], mix_norm[l], mode="next_norm",
                                      name="ffn1", side=(mix_w_in[l],))

        (proj,), (ffn2_w_in_b, ffn2_w_out_b, glu_w_b, conv_w_out_b, mix_w_out_b) = _mm(
            un, mix_w_in_b, name="mix_in",
            side=(ffn2_w_in[l], ffn2_w_out[l], ssm_glu_w[l], conv_w_out[l], mix_w_out[l]))

        ys = _ssm(proj, ssm_a_re[l], ssm_a_im[l], ssm_log_dt[l], ssm_b_re[l], ssm_b_im[l],
                  ssm_c_re[l], ssm_c_im[l], ssm_d[l], n_chunks=n_chunks, name="ssm")

        cz = _conv(proj, conv_w[l], batch=batch, seq=seq, d_conv=d_conv, col0=d_ssm, name="conv")

        (merged,), (wq_b, wk_b, wv_b, wo_b) = _merge(
            ys, cz, glu_w_b, conv_w_out_b, proj, d_model=d_model, gate_col0=d_ssm + 3 * d_conv, name="merge",
            side=(xattn_wq[l], xattn_wk[l], xattn_wv[l], xattn_wo[l]))
        h, h_b = _mm_res(merged, mix_w_out_b, h, name="mix_out", bf16_copy=True)

        k, v = _norm_mm(memf, mem_norm[l], (wk_b, wv_b), name="xattn_kv", bn=512)
        o = _q_attn(h_b, xattn_norm[l], wq_b, k, v, batch=batch, seq=seq, n_mem=n_mem, n_heads=n_heads,
                    name="xattn")
        (h,) = _mm_res(o, wo_b, h, name="xattn_o")

        h = _ffn_loop(h, ffn2_norm[l], ffn2_w_in_b, ffn2_w_out_b, final_norm,
                      mode="final_norm" if last else "plain", name="ffn2")
    if depth == 0:
        raise NotImplementedError("depth 0")
    return h.reshape(batch, seq, d_model)
```

```python
import functools

import jax
import jax.numpy as jnp
from jax import lax
from jax.experimental import pallas as pl
from jax.experimental.pallas import tpu as pltpu

F32 = jnp.float32
BF16 = jnp.bfloat16
RMS_EPS = 1e-6

V7X_LANES = 128
V7X_MXU_DIM = 256
V7X_VMEM_BYTES = 64 * 1024 * 1024
VMEM_LIMIT = V7X_VMEM_BYTES - 4 * 1024 * 1024

SSM_CHUNK = 16
SSM_GROUPS_PER_BLOCK = 16
SSM_SUB = 4


def _params(dims):
    return pltpu.CompilerParams(dimension_semantics=dims, vmem_limit_bytes=VMEM_LIMIT)


def _blk(dim, pref):
    b = min(dim, pref)
    assert dim % b == 0, (dim, pref)
    return b


def _rms_rows(x, g):
    ms = jnp.mean(x * x, axis=-1, keepdims=True)
    return x * lax.rsqrt(ms + RMS_EPS) * g


def _row_chunks(n_rows, chunk, body):
    chunk = min(chunk, n_rows)
    assert n_rows % chunk == 0

    def step(c, carry):
        body(pl.ds(pl.multiple_of(c * chunk, chunk), chunk))
        return carry

    lax.fori_loop(0, n_rows // chunk, step, 0)


BF16_SUBLANES = 16


def _side_spec(shape, grid):
    r, c = shape
    n_steps = grid[0] * grid[1]
    best = None
    for rb in range(BF16_SUBLANES, r + 1, BF16_SUBLANES):
        if r % rb:
            continue
        for cb in range(V7X_LANES, c + 1, V7X_LANES):
            if c % cb == 0 and (r // rb) * (c // cb) <= n_steps and (best is None or rb * cb < best[0] * best[1]):
                best = (rb, cb)
    assert best is not None, (shape, grid)
    rb, cb = best
    ncb = c // cb
    last = (r // rb) * ncb - 1

    def index(i, j):
        blk = jnp.minimum(i * grid[1] + j, last)
        return blk // ncb, blk % ncb

    return pl.BlockSpec((rb, cb), index)


def _call(body, *, grid, in_specs, out_specs, out_shapes, scratch_shapes=(), args, side=(), name):
    n_in, n_out, n_side = len(in_specs), len(out_specs), len(side)

    def kernel(*refs):
        side_in = refs[n_in:n_in + n_side]
        outs = refs[n_in + n_side:n_in + n_side + n_out]
        side_out = refs[n_in + n_side + n_out:n_in + 2 * n_side + n_out]

        def side_cast():
            for src, dst in zip(side_in, side_out):
                dst[...] = src[...].astype(BF16)

        body(*refs[:n_in], *outs, *refs[n_in + 2 * n_side + n_out:], side_cast=side_cast)

    side_specs = [_side_spec(w.shape, grid) for w in side]
    res = pl.pallas_call(
        kernel,
        grid=grid,
        in_specs=list(in_specs) + side_specs,
        out_specs=list(out_specs) + side_specs,
        out_shape=list(out_shapes) + [jax.ShapeDtypeStruct(w.shape, BF16) for w in side],
        scratch_shapes=list(scratch_shapes),
        compiler_params=_params(("arbitrary",) * len(grid)),
        name=name,
    )(*args, *side)
    return tuple(res[:n_out]), tuple(res[n_out:])


def _ffn_kernel(x_ref, g_ref, wa_ref, wb_ref, wout_ref, g2_ref, *rest, mode, side_cast):
    if mode == "next_norm":
        o_ref, hn_ref, xn_ref = rest
    else:
        o_ref, xn_ref = rest
    f = pl.program_id(1)
    bm = x_ref.shape[0]

    @pl.when(f == 0)
    def _():
        def body(rows):
            xn_ref[rows, :] = _rms_rows(x_ref[rows, :], g_ref[...]).astype(BF16)
            o_ref[rows, :] = jnp.zeros((rows.size, o_ref.shape[1]), F32)

        _row_chunks(bm, 64, body)

    xn = xn_ref[...]
    a = jnp.dot(xn, wa_ref[...], preferred_element_type=F32)
    b = jnp.dot(xn, wb_ref[...], preferred_element_type=F32)
    gated = (a * jax.nn.sigmoid(a) * b).astype(BF16)
    side_cast()
    d = o_ref.shape[1]
    bn = min(d, 1024)
    for n0 in range(0, d, bn):
        o_ref[:, n0:n0 + bn] += jnp.dot(gated, wout_ref[:, n0:n0 + bn], preferred_element_type=F32)

    @pl.when(f == pl.num_programs(1) - 1)
    def _():
        def body(rows):
            h = x_ref[rows, :] + 0.5 * o_ref[rows, :]
            if mode == "final_norm":
                h = _rms_rows(h, g2_ref[...])
            elif mode == "next_norm":
                hn_ref[rows, :] = _rms_rows(h, g2_ref[...]).astype(BF16)
            o_ref[rows, :] = h

        _row_chunks(bm, 64, body)


def _ffn(x, g, w_in, w_out, g2, *, mode, name, side=()):
    t, d = x.shape
    f_dim = w_out.shape[0]
    bf = _blk(f_dim, V7X_MXU_DIM)
    nf = f_dim // bf
    bm = _blk(t, 512)
    w_in = w_in.astype(BF16)
    row_spec = pl.BlockSpec((bm, d), lambda i, f: (i, 0))
    out_specs, out_shapes = [row_spec], [jax.ShapeDtypeStruct((t, d), F32)]
    if mode == "next_norm":
        out_specs, out_shapes = out_specs + [row_spec], out_shapes + [jax.ShapeDtypeStruct((t, d), BF16)]
    return _call(
        functools.partial(_ffn_kernel, mode=mode),
        grid=(t // bm, nf),
        in_specs=[
            row_spec,
            pl.BlockSpec((1, d), lambda i, f: (0, 0)),
            pl.BlockSpec((d, bf), lambda i, f: (0, f)),
            pl.BlockSpec((d, bf), lambda i, f: (0, nf + f)),
            pl.BlockSpec((bf, d), lambda i, f: (f, 0)),
            pl.BlockSpec((1, d), lambda i, f: (0, 0)),
        ],
        out_specs=out_specs,
        out_shapes=out_shapes,
        scratch_shapes=[pltpu.VMEM((bm, d), BF16)],
        args=(x, g.reshape(1, d), w_in, w_in, w_out.astype(BF16), g2.reshape(1, d)),
        side=side,
        name=name,
    )


def _ffn_big_kernel(g_ref, g2_ref, x_hbm, win_hbm, wout_hbm, o_hbm, acc_ref, stage_ref, xn_ref,
                    wa_buf, wb_buf, wo_buf, wsem, iosem, *, mode, nf, bf):
    i = pl.program_id(0)
    n_blocks = pl.num_programs(0)
    bm, d = acc_ref.shape

    def rows_of(blk):
        return pl.ds(pl.multiple_of(blk * bm, bm), bm)

    def x_copy(blk):
        return pltpu.make_async_copy(x_hbm.at[rows_of(blk), :], stage_ref, iosem.at[0])

    def o_copy(blk):
        return pltpu.make_async_copy(acc_ref, o_hbm.at[rows_of(blk), :], iosem.at[1])

    def w_copies(f, slot):
        return (
            pltpu.make_async_copy(win_hbm.at[:, pl.ds(f * bf, bf)], wa_buf.at[slot], wsem.at[0, slot]),
            pltpu.make_async_copy(win_hbm.at[:, pl.ds((nf + f) * bf, bf)], wb_buf.at[slot], wsem.at[1, slot]),
            pltpu.make_async_copy(wout_hbm.at[pl.ds(f * bf, bf), :], wo_buf.at[slot], wsem.at[2, slot]),
        )

    def start(f, slot):
        for c in w_copies(f, slot):
            c.start()

    def wait(f, slot):
        for c in w_copies(f, slot):
            c.wait()

    start(0, 0)

    @pl.when(i == 0)
    def _():
        x_copy(0).start()

    x_copy(i).wait()

    def normalise(rows):
        xn_ref[rows, :] = _rms_rows(stage_ref[rows, :], g_ref[...]).astype(BF16)

    _row_chunks(bm, 64, normalise)

    @pl.when(i > 0)
    def _():
        o_copy(i - 1).wait()

    def init(rows):
        acc_ref[rows, :] = 2.0 * stage_ref[rows, :]

    _row_chunks(bm, 64, init)

    @pl.when(i + 1 < n_blocks)
    def _():
        x_copy(i + 1).start()

    def tile(slot):
        xn = xn_ref[...]
        a = jnp.dot(xn, wa_buf[slot], preferred_element_type=F32)
        b = jnp.dot(xn, wb_buf[slot], preferred_element_type=F32)
        gated = (a * jax.nn.sigmoid(a) * b).astype(BF16)
        bn = min(d, 2 * V7X_MXU_DIM)
        for n0 in range(0, d, bn):
            acc_ref[:, n0:n0 + bn] += jnp.dot(gated, wo_buf[slot, :, n0:n0 + bn], preferred_element_type=F32)

    n_pairs = (nf - 1) // 2

    def pair(p, carry):
        f0 = 2 * p
        start(f0 + 1, 1)
        wait(f0, 0)
        tile(0)
        start(f0 + 2, 0)
        wait(f0 + 1, 1)
        tile(1)
        return carry

    lax.fori_loop(0, n_pairs, pair, 0)
    f_last = 2 * n_pairs
    if nf - f_last == 2:
        start(f_last + 1, 1)
    wait(f_last, 0)
    tile(0)
    if nf - f_last == 2:
        wait(f_last + 1, 1)
        tile(1)

    def finish(rows):
        h = 0.5 * acc_ref[rows, :]
        if mode == "final_norm":
            h = _rms_rows(h, g2_ref[...])
        acc_ref[rows, :] = h

    _row_chunks(bm, 64, finish)
    o_copy(i).start()

    @pl.when(i == n_blocks - 1)
    def _():
        o_copy(i).wait()


def _ffn_big(x, g, w_in, w_out, g2, *, mode, name):
    t, d = x.shape
    f_dim = w_out.shape[0]
    bf = _blk(f_dim, V7X_MXU_DIM)
    nf = f_dim // bf
    bm = _blk(t, 1024)
    vec_spec = pl.BlockSpec((1, d), lambda i: (0, 0))
    hbm = pl.BlockSpec(memory_space=pl.ANY)
    return pl.pallas_call(
        functools.partial(_ffn_big_kernel, mode=mode, nf=nf, bf=bf),
        grid=(t // bm,),
        in_specs=[vec_spec, vec_spec, hbm, hbm, hbm],
        out_specs=hbm,
        out_shape=jax.ShapeDtypeStruct((t, d), F32),
        scratch_shapes=[
            pltpu.VMEM((bm, d), F32),
            pltpu.VMEM((bm, d), F32),
            pltpu.VMEM((bm, d), BF16),
            pltpu.VMEM((2, d, bf), BF16),
            pltpu.VMEM((2, d, bf), BF16),
            pltpu.VMEM((2, bf, d), BF16),
            pltpu.SemaphoreType.DMA((3, 2)),
            pltpu.SemaphoreType.DMA((2,)),
        ],
        compiler_params=_params(("arbitrary",)),
        name=name,
    )(g.reshape(1, d), g2.reshape(1, d), x, w_in.astype(BF16), w_out.astype(BF16))


def _ffn_loop_kernel(x_ref, g_ref, g2_ref, win_hbm, wout_hbm, o_ref, xn_ref, wa_buf, wb_buf, wo_buf, sem,
                     *, mode, nf, bf):
    bm, d = x_ref.shape

    def copies_in(f, slot):
        return (
            pltpu.make_async_copy(win_hbm.at[:, pl.ds(f * bf, bf)], wa_buf.at[slot], sem.at[0, slot]),
            pltpu.make_async_copy(win_hbm.at[:, pl.ds((nf + f) * bf, bf)], wb_buf.at[slot], sem.at[1, slot]),
        )

    def copies_out(f, slot):
        return (pltpu.make_async_copy(wout_hbm.at[pl.ds(f * bf, bf), :], wo_buf.at[pl.ds(slot * bf, bf), :],
                                      sem.at[2, slot]),)

    def start(copies):
        for c in copies:
            c.start()

    def wait(copies):
        for c in copies:
            c.wait()

    start(copies_in(0, 0))
    start(copies_out(0, 0))

    def prologue(rows):
        xn_ref[rows, :] = _rms_rows(x_ref[rows, :], g_ref[...]).astype(BF16)
        o_ref[rows, :] = jnp.zeros((rows.size, d), F32)

    _row_chunks(bm, 64, prologue)

    def gate(slot):
        xn = xn_ref[...]
        a = jnp.dot(xn, wa_buf[slot], preferred_element_type=F32)
        b = jnp.dot(xn, wb_buf[slot], preferred_element_type=F32)
        return (a * jax.nn.sigmoid(a) * b).astype(BF16)

    def project(gated, w_rows):
        bn = min(d, 1024)
        for n0 in range(0, d, bn):
            o_ref[:, n0:n0 + bn] += jnp.dot(gated, wo_buf[w_rows, n0:n0 + bn], preferred_element_type=F32)

    def two_tiles(f0, more):
        start(copies_in(f0 + 1, 1))
        start(copies_out(f0 + 1, 1))
        wait(copies_in(f0, 0))
        g0 = gate(0)
        if more:
            start(copies_in(f0 + 2, 0))
        wait(copies_in(f0 + 1, 1))
        g1 = gate(1)
        wait(copies_out(f0, 0))
        wait(copies_out(f0 + 1, 1))
        project(jnp.concatenate([g0, g1], axis=1), slice(0, 2 * bf))
        if more:
            start(copies_out(f0 + 2, 0))

    n_pairs = (nf - 1) // 2

    def pair(p, carry):
        two_tiles(2 * p, True)
        return carry

    lax.fori_loop(0, n_pairs, pair, 0)
    f_last = 2 * n_pairs
    if nf - f_last == 2:
        two_tiles(f_last, False)
    else:
        wait(copies_in(f_last, 0))
        g0 = gate(0)
        wait(copies_out(f_last, 0))
        project(g0, slice(0, bf))

    def epilogue(rows):
        h = x_ref[rows, :] + 0.5 * o_ref[rows, :]
        if mode == "final_norm":
            h = _rms_rows(h, g2_ref[...])
        o_ref[rows, :] = h

    _row_chunks(bm, 64, epilogue)


def _ffn_loop(x, g, w_in, w_out, g2, *, mode, name):
    t, d = x.shape
    f_dim = w_out.shape[0]
    bf = _blk(f_dim, V7X_MXU_DIM)
    nf = f_dim // bf
    bm = _blk(t, 512)
    row_spec = pl.BlockSpec((bm, d), lambda i: (i, 0))
    vec_spec = pl.BlockSpec((1, d), lambda i: (0, 0))
    return pl.pallas_call(
        functools.partial(_ffn_loop_kernel, mode=mode, nf=nf, bf=bf),
        grid=(t // bm,),
        in_specs=[row_spec, vec_spec, vec_spec,
                  pl.BlockSpec(memory_space=pl.ANY), pl.BlockSpec(memory_space=pl.ANY)],
        out_specs=row_spec,
        out_shape=jax.ShapeDtypeStruct((t, d), F32),
        scratch_shapes=[
            pltpu.VMEM((bm, d), BF16),
            pltpu.VMEM((2, d, bf), BF16),
            pltpu.VMEM((2, d, bf), BF16),
            pltpu.VMEM((2 * bf, d), BF16),
            pltpu.SemaphoreType.DMA((3, 2)),
        ],
        compiler_params=_params(("arbitrary",)),
        name=name,
    )(x, g.reshape(1, d), g2.reshape(1, d), w_in.astype(BF16), w_out.astype(BF16))


def _mm_kernel(a_ref, w_ref, o_ref, *, side_cast):
    o_ref[...] = jnp.dot(a_ref[...], w_ref[...], preferred_element_type=F32).astype(o_ref.dtype)
    side_cast()


def _mm(a, w, *, name, bm=1024, bn=1024, side=()):
    m, k = a.shape
    n = w.shape[1]
    bm = _blk(m, bm)
    bn = _blk(n, bn)
    return _call(
        _mm_kernel,
        grid=(m // bm, n // bn),
        in_specs=[
            pl.BlockSpec((bm, k), lambda i, j: (i, 0)),
            pl.BlockSpec((k, bn), lambda i, j: (0, j)),
        ],
        out_specs=[pl.BlockSpec((bm, bn), lambda i, j: (i, j))],
        out_shapes=[jax.ShapeDtypeStruct((m, n), BF16)],
        args=(a, w.astype(BF16)),
        side=side,
        name=name,
    )


def _norm_mm_kernel(a_ref, g_ref, *rest, n_w):
    w_refs, o_refs, an_ref = rest[:n_w], rest[n_w:2 * n_w], rest[2 * n_w]

    @pl.when(pl.program_id(1) == 0)
    def _():
        def body(rows):
            an_ref[rows, :] = _rms_rows(a_ref[rows, :].astype(F32), g_ref[...]).astype(BF16)

        _row_chunks(a_ref.shape[0], 64, body)

    for w_ref, o_ref in zip(w_refs, o_refs):
        o_ref[...] = jnp.dot(an_ref[...], w_ref[...], preferred_element_type=F32).astype(o_ref.dtype)


def _norm_mm(a, g, ws, *, name, bm=512, bn=1024):
    m, k = a.shape
    n = ws[0].shape[1]
    bm = _blk(m, bm)
    bn = _blk(n, bn)
    tile = pl.BlockSpec((bm, bn), lambda i, j: (i, j))
    return pl.pallas_call(
        functools.partial(_norm_mm_kernel, n_w=len(ws)),
        grid=(m // bm, n // bn),
        in_specs=[
            pl.BlockSpec((bm, k), lambda i, j: (i, 0)),
            pl.BlockSpec((1, k), lambda i, j: (0, 0)),
        ] + [pl.BlockSpec((k, bn), lambda i, j: (0, j))] * len(ws),
        out_specs=[tile] * len(ws),
        out_shape=[jax.ShapeDtypeStruct((m, n), BF16)] * len(ws),
        scratch_shapes=[pltpu.VMEM((bm, k), BF16)],
        compiler_params=_params(("arbitrary", "arbitrary")),
        name=name,
    )(a, g.reshape(1, k), *[w.astype(BF16) for w in ws])


def _mm_res_kernel(a_ref, w_ref, res_ref, o_ref, *copy_ref):
    bn = o_ref.shape[1]
    cn = min(bn, 2 * V7X_MXU_DIM)
    for n0 in range(0, bn, cn):
        cols = slice(n0, n0 + cn)
        out = res_ref[:, cols] + jnp.dot(a_ref[...], w_ref[:, cols], preferred_element_type=F32)
        o_ref[:, cols] = out
        for ref in copy_ref:
            ref[:, cols] = out.astype(ref.dtype)


def _mm_res(a, w, res, *, name, bm=1024, bn=1024, bf16_copy=False):
    m, k = a.shape
    n = w.shape[1]
    bm = _blk(m, bm)
    bn = _blk(n, bn)
    tile = pl.BlockSpec((bm, bn), lambda i, j: (i, j))
    return pl.pallas_call(
        _mm_res_kernel,
        grid=(m // bm, n // bn),
        in_specs=[
            pl.BlockSpec((bm, k), lambda i, j: (i, 0)),
            pl.BlockSpec((k, bn), lambda i, j: (0, j)),
            tile,
        ],
        out_specs=[tile] + [tile] * bf16_copy,
        out_shape=[jax.ShapeDtypeStruct((m, n), F32)] + [jax.ShapeDtypeStruct((m, n), BF16)] * bf16_copy,
        compiler_params=_params(("parallel", "arbitrary")),
        name=name,
    )(a, w.astype(BF16), res)


def _cmul(ar, ai, br, bi):
    return ar * br - ai * bi, ar * bi + ai * br


def _gelu_exact(x):
    return 0.5 * x * (1.0 + lax.erf(x * (2.0 ** -0.5)))


def _ssm_kernel(tok_ref, are_ref, aim_ref, ldt_ref, braw_ref, craw_ref, dsk_ref, o_ref,
                bstack_ref, nst_ref, dstack_ref, pw_ref, apow_ref, sre_ref, sim_ref, w_ref,
                slab_ref, u_ref, *, n_seq, n_chunks, n_log):
    w256 = V7X_MXU_DIM
    ns = braw_ref.shape[2] // 2
    n_strip = ns // V7X_LANES
    off = sre_ref.shape[2] - n_chunks
    rows = n_seq * n_chunks
    n_slab = w256 // V7X_LANES

    for hf in range(n_slab):
        slab_ref[hf] = tok_ref[:, hf * V7X_LANES:(hf + 1) * V7X_LANES].astype(F32)
    for r in range(SSM_CHUNK):
        parts = [slab_ref[hf, pl.ds(r, rows, stride=SSM_CHUNK), :] for hf in range(n_slab)]
        u_ref[r] = jnp.concatenate(parts, axis=1).astype(BF16)

    @pl.when(pl.program_id(1) == 0)
    def _prepare():
        ar = are_ref[0]
        ai = aim_ref[0]
        dt = jnp.exp(ldt_ref[0])
        mag = jnp.exp(ar * dt)
        lr = mag * jnp.cos(ai * dt)
        li = mag * jnp.sin(ai * dt)
        den = ar * ar + ai * ai
        zr = ((lr - 1.0) * ar + li * ai) / den
        zi = (li * ar - (lr - 1.0) * ai) / den
        sr, si = _cmul(braw_ref[0, :, :ns], braw_ref[0, :, ns:], zr, zi)
        cr = craw_ref[0, :, :ns]
        ci = craw_ref[0, :, ns:]
        cn = jnp.concatenate([cr, -ci], axis=1).astype(BF16)
        for j in range(SSM_CHUNK):
            bs = jnp.concatenate([sr, si], axis=1).astype(BF16)
            dj = lax.dot_general(bs, cn, (((1,), (1,)), ((), ())), preferred_element_type=F32)
            dstack_ref[(SSM_CHUNK - 1 - j) * w256:(SSM_CHUNK - j) * w256, :] = dj.astype(BF16)
            if j < SSM_SUB:
                bstack_ref[(SSM_SUB - 1 - j) * w256:(SSM_SUB - j) * w256, :] = bs
            sr, si = _cmul(sr, si, lr, li)
        mr, mi = lr, li
        for s in range(SSM_SUB):
            er, ei = _cmul(cr, ci, mr, mi)
            nst_ref[s * w256:(s + 1) * w256, :] = jnp.concatenate([er, -ei], axis=1).astype(BF16)
            if s < SSM_SUB - 1:
                mr, mi = _cmul(mr, mi, lr, li)
        pw_ref[0:1, :] = mr
        pw_ref[1:2, :] = mi
        p8r, p8i = _cmul(mr, mi, mr, mi)
        kr, ki = _cmul(p8r, p8i, p8r, p8i)
        for k in range(n_log):
            for s in range(n_strip):
                apow_ref[k, s, 0:1, :] = kr[:, s * V7X_LANES:(s + 1) * V7X_LANES]
                apow_ref[k, s, 1:2, :] = ki[:, s * V7X_LANES:(s + 1) * V7X_LANES]
            kr, ki = _cmul(kr, ki, kr, ki)
        zeros = jnp.zeros((off, V7X_LANES), F32)
        for s in range(n_strip):
            for q in range(n_seq):
                sre_ref[s, q, 0:off, :] = zeros
                sim_ref[s, q, 0:off, :] = zeros

    p4r = pw_ref[0:1, :]
    p4i = pw_ref[1:2, :]

    xr = xi = None
    for q in range(SSM_SUB):
        z = None
        for s in range(SSM_SUB):
            d = jnp.dot(u_ref[SSM_SUB * q + s], bstack_ref[s * w256:(s + 1) * w256, :],
                        preferred_element_type=F32)
            z = d if z is None else z + d
        if xr is None:
            xr, xi = z[:, :ns], z[:, ns:]
        else:
            xr, xi = _cmul(xr, xi, p4r, p4i)
            xr, xi = xr + z[:, :ns], xi + z[:, ns:]

    for s in range(n_strip):
        lanes = slice(s * V7X_LANES, (s + 1) * V7X_LANES)
        for q in range(n_seq):
            sre_ref[s, q, off:, :] = xr[q * n_chunks:(q + 1) * n_chunks, lanes]
            sim_ref[s, q, off:, :] = xi[q * n_chunks:(q + 1) * n_chunks, lanes]
    for s in range(n_strip):
        for q in range(n_seq):
            for k in range(n_log):
                sh = 1 << k
                kr = apow_ref[k, s, 0:1, :]
                ki = apow_ref[k, s, 1:2, :]
                pr = sre_ref[s, q, off - sh:off - sh + n_chunks, :]
                pi = sim_ref[s, q, off - sh:off - sh + n_chunks, :]
                tr, ti = _cmul(pr, pi, kr, ki)
                sre_ref[s, q, off:, :] = sre_ref[s, q, off:, :] + tr
                sim_ref[s, q, off:, :] = sim_ref[s, q, off:, :] + ti

    ysts = []
    for q in range(SSM_SUB):
        for s in range(n_strip):
            lanes_r = slice(s * V7X_LANES, (s + 1) * V7X_LANES)
            lanes_i = slice(ns + s * V7X_LANES, ns + (s + 1) * V7X_LANES)
            for sq in range(n_seq):
                rs = slice(sq * n_chunks, (sq + 1) * n_chunks)
                if q == 0:
                    vr = sre_ref[s, sq, off - 1:off - 1 + n_chunks, :]
                    vi = sim_ref[s, sq, off - 1:off - 1 + n_chunks, :]
                else:
                    vr, vi = _cmul(sre_ref[s, sq, off:, :], sim_ref[s, sq, off:, :],
                                   p4r[:, lanes_r], p4i[:, lanes_r])
                if q < SSM_SUB - 1:
                    sre_ref[s, sq, off:, :] = vr
                    sim_ref[s, sq, off:, :] = vi
                w_ref[rs, lanes_r] = vr.astype(BF16)
                w_ref[rs, lanes_i] = vi.astype(BF16)
        ysts.append(lax.dot_general(w_ref[...], nst_ref[...], (((1,), (1,)), ((), ())),
                                    preferred_element_type=F32))

    dsk = dsk_ref[0]
    for t in range(SSM_CHUNK):
        y = ysts[t // SSM_SUB][:, (t % SSM_SUB) * w256:(t % SSM_SUB + 1) * w256]
        y = y + dsk * u_ref[t].astype(F32)
        for r in range(t + 1):
            blk = SSM_CHUNK - 1 - t + r
            y = y + jnp.dot(u_ref[r], dstack_ref[blk * w256:(blk + 1) * w256, :],
                            preferred_element_type=F32)
        y = _gelu_exact(y)
        for hf in range(n_slab):
            slab_ref[hf, pl.ds(t, rows, stride=SSM_CHUNK), :] = y[:, hf * V7X_LANES:(hf + 1) * V7X_LANES]
    for hf in range(n_slab):
        o_ref[:, hf * V7X_LANES:(hf + 1) * V7X_LANES] = slab_ref[hf].astype(o_ref.dtype)


def _ssm(proj, a_re, a_im, log_dt, b_re, b_im, c_re, c_im, d_skip, *, n_chunks, name):
    n_tok = proj.shape[0]
    d_ssm = d_skip.shape[0]
    n_rows = n_tok // SSM_CHUNK
    n_groups, n_state = a_re.shape
    h = d_ssm // n_groups
    gpb = SSM_GROUPS_PER_BLOCK
    assert gpb * h == V7X_MXU_DIM and n_groups % gpb == 0
    n_blocks = n_groups // gpb
    ns = gpb * n_state
    n_seq = 1
    n_log = max(1, (n_chunks - 1).bit_length())
    off = max(8, 1 << (n_log - 1))

    col_group = (jnp.arange(2 * ns) % ns) // n_state
    same_group = (jnp.arange(gpb)[:, None] == col_group[None, :]).astype(F32)

    def block_diag(x_re, x_im):
        lanes = lambda x: x.reshape(n_blocks, gpb, h, n_state).transpose(0, 2, 1, 3).reshape(n_blocks, h, ns)
        x = jnp.concatenate([lanes(x_re), lanes(x_im)], axis=2)
        x = x[:, None, :, :] * same_group[None, :, None, :]
        return x.reshape(n_blocks, gpb * h, 2 * ns)

    braw = block_diag(b_re.transpose(0, 2, 1), b_im.transpose(0, 2, 1))
    craw = block_diag(c_re, c_im)
    vec = lambda x: x.reshape(n_blocks, 1, ns)
    ldt = jnp.broadcast_to(log_dt[:, None], (n_groups, n_state))
    rows = n_seq * n_chunks
    kernel = functools.partial(_ssm_kernel, n_seq=n_seq, n_chunks=n_chunks, n_log=n_log)
    vspec = pl.BlockSpec((1, 1, ns), lambda g, r: (g, 0, 0))
    mspec = pl.BlockSpec((1, V7X_MXU_DIM, 2 * ns), lambda g, r: (g, 0, 0))
    uspec = pl.BlockSpec((rows * SSM_CHUNK, V7X_MXU_DIM), lambda g, r: (r, g))
    return pl.pallas_call(
        kernel,
        grid=(n_blocks, n_rows // rows),
        in_specs=[uspec, vspec, vspec, vspec, mspec, mspec,
                  pl.BlockSpec((1, 1, V7X_MXU_DIM), lambda g, r: (g, 0, 0))],
        out_specs=uspec,
        out_shape=jax.ShapeDtypeStruct((n_tok, d_ssm), BF16),
        scratch_shapes=[
            pltpu.VMEM((SSM_SUB * V7X_MXU_DIM, 2 * ns), BF16),
            pltpu.VMEM((SSM_SUB * V7X_MXU_DIM, 2 * ns), BF16),
            pltpu.VMEM((SSM_CHUNK * V7X_MXU_DIM, V7X_MXU_DIM), BF16),
            pltpu.VMEM((8, ns), F32),
            pltpu.VMEM((n_log, ns // V7X_LANES, 8, V7X_LANES), F32),
            pltpu.VMEM((ns // V7X_LANES, n_seq, off + n_chunks, V7X_LANES), F32),
            pltpu.VMEM((ns // V7X_LANES, n_seq, off + n_chunks, V7X_LANES), F32),
            pltpu.VMEM((rows, 2 * ns), BF16),
            pltpu.VMEM((V7X_MXU_DIM // V7X_LANES, rows * SSM_CHUNK, V7X_LANES), F32),
            pltpu.VMEM((SSM_CHUNK, rows, V7X_MXU_DIM), BF16),
        ],
        compiler_params=_params(("parallel", "arbitrary")),
        name=name,
    )(proj, vec(a_re), vec(a_im), vec(ldt), braw, craw, d_skip.reshape(n_blocks, 1, V7X_MXU_DIM))


def _conv_kernel(cb_ref, cc_ref, ch_ref, w_ref, o_ref, z_ref, *, chunk):
    seq = cb_ref.shape[0]
    pad = z_ref.shape[0] - seq
    kw = w_ref.shape[0]
    z_ref[0:pad, :] = jnp.zeros((pad, z_ref.shape[1]), F32)

    def fill(rows):
        z_ref[pl.ds(rows.start + pad, rows.size), :] = cc_ref[rows, :].astype(F32) * ch_ref[rows, :].astype(F32)

    _row_chunks(seq, chunk, fill)
    for c in range(seq // chunk):
        r0 = c * chunk
        acc = None
        for k in range(kw):
            lag = kw - 1 - k
            term = w_ref[k:k + 1, :] * z_ref[pad + r0 - lag:pad + r0 - lag + chunk, :]
            acc = term if acc is None else acc + term
        o_ref[r0:r0 + chunk, :] = (cb_ref[r0:r0 + chunk, :].astype(F32) * acc).astype(o_ref.dtype)


def _conv(proj, conv_w, *, batch, seq, d_conv, col0, name):
    bc = _blk(d_conv, V7X_MXU_DIM)
    nb = d_conv // bc
    chunk = min(seq, 512)
    kw = conv_w.shape[0]
    cspec = lambda k: pl.BlockSpec((seq, bc), lambda b, j: (b, (col0 + k * d_conv) // bc + j))
    return pl.pallas_call(
        functools.partial(_conv_kernel, chunk=chunk),
        grid=(batch, nb),
        in_specs=[cspec(0), cspec(1), cspec(2), pl.BlockSpec((kw, bc), lambda b, j: (0, j))],
        out_specs=pl.BlockSpec((seq, bc), lambda b, j: (b, j)),
        out_shape=jax.ShapeDtypeStruct((batch * seq, d_conv), BF16),
        scratch_shapes=[pltpu.VMEM((8 + seq, bc), F32)],
        compiler_params=_params(("parallel", "parallel")),
        name=name,
    )(proj, proj, proj, conv_w)


def _merge_kernel(ys_ref, cz_ref, wv_ref, wg_ref, wc_ref, ga_ref, gb_ref, o_ref, *, side_cast):
    side_cast()
    ys = ys_ref[...]
    val = jnp.dot(ys, wv_ref[...], preferred_element_type=F32)
    gate = jnp.dot(ys, wg_ref[...], preferred_element_type=F32)
    yb = jnp.dot(cz_ref[...], wc_ref[...], preferred_element_type=F32)
    ya = val * jax.nn.sigmoid(gate)
    out = jax.nn.sigmoid(ga_ref[...].astype(F32)) * ya + jax.nn.sigmoid(gb_ref[...].astype(F32)) * yb
    o_ref[...] = out.astype(o_ref.dtype)


def _merge(ys, cz, glu_w, conv_w_out, proj, *, d_model, gate_col0, name, bm=1024, bn=512, side=()):
    t, k = ys.shape
    bm = _blk(t, bm)
    bn = _blk(d_model, bn)
    nj = d_model // bn
    aspec = pl.BlockSpec((bm, k), lambda i, j: (i, 0))
    wspec = lambda off: pl.BlockSpec((k, bn), lambda i, j: (0, off + j))
    gspec = lambda off: pl.BlockSpec((bm, bn), lambda i, j: (i, off + j))
    glu_w = glu_w.astype(BF16)
    return _call(
        _merge_kernel,
        grid=(t // bm, nj),
        in_specs=[aspec, aspec, wspec(0), wspec(nj), wspec(0),
                  gspec(gate_col0 // bn), gspec(gate_col0 // bn + nj)],
        out_specs=[pl.BlockSpec((bm, bn), lambda i, j: (i, j))],
        out_shapes=[jax.ShapeDtypeStruct((t, d_model), BF16)],
        args=(ys, cz, glu_w, glu_w, conv_w_out.astype(BF16), proj, proj),
        side=side,
        name=name,
    )


def _attend(q, k, v, scale):
    s = lax.dot_general(q.astype(BF16), k, (((1,), (1,)), ((), ())), preferred_element_type=F32) * scale
    p = jnp.exp(s - jnp.max(s, axis=-1, keepdims=True))
    denom = jnp.sum(p, axis=-1, keepdims=True)
    return jnp.dot(p.astype(BF16), v, preferred_element_type=F32) / denom


def _q_attn_kernel(a_ref, g_ref, wq_ref, k_ref, v_ref, o_ref, an_ref, *, scale, row_chains):
    @pl.when(pl.program_id(1) == 0)
    def _():
        def body(rows):
            an_ref[rows, :] = _rms_rows(a_ref[rows, :].astype(F32), g_ref[...]).astype(BF16)

        _row_chunks(a_ref.shape[0], 64, body)

    rc = a_ref.shape[0] // row_chains
    for r0 in range(0, a_ref.shape[0], rc):
        q = jnp.dot(an_ref[r0:r0 + rc, :], wq_ref[...], preferred_element_type=F32)
        o_ref[r0:r0 + rc, :] = _attend(q, k_ref[...], v_ref[...], scale).astype(o_ref.dtype)


def _q_attn(a, g, wq, k, v, *, batch, seq, n_mem, n_heads, name, bm=1024, row_chains=2):
    t, d = a.shape
    dh = d // n_heads
    bm = _blk(seq, bm)
    per_seq = seq // bm
    return pl.pallas_call(
        functools.partial(_q_attn_kernel, scale=dh ** -0.5, row_chains=row_chains),
        grid=(t // bm, n_heads),
        in_specs=[
            pl.BlockSpec((bm, d), lambda i, h: (i, 0)),
            pl.BlockSpec((1, d), lambda i, h: (0, 0)),
            pl.BlockSpec((d, dh), lambda i, h: (0, h)),
            pl.BlockSpec((n_mem, dh), lambda i, h: (i // per_seq, h)),
            pl.BlockSpec((n_mem, dh), lambda i, h: (i // per_seq, h)),
        ],
        out_specs=pl.BlockSpec((bm, dh), lambda i, h: (i, h)),
        out_shape=jax.ShapeDtypeStruct((t, d), BF16),
        scratch_shapes=[pltpu.VMEM((bm, d), BF16)],
        compiler_params=_params(("arbitrary", "arbitrary")),
        name=name,
    )(a, g.reshape(1, d), wq.astype(BF16), k, v)


def kernel(x, mem, ffn1_norm, ffn1_w_in, ffn1_w_out, mix_norm, mix_w_in, ssm_a_re, ssm_a_im, ssm_log_dt, ssm_b_re, ssm_b_im, ssm_c_re, ssm_c_im, ssm_d, ssm_glu_w, conv_w, conv_w_out, mix_w_out, xattn_norm, mem_norm, xattn_wq, xattn_wk, xattn_wv, xattn_wo, ffn2_norm, ffn2_w_in, ffn2_w_out, final_norm):
    batch, seq, d_model = x.shape
    n_mem = mem.shape[1]
    depth = ffn1_norm.shape[0]
    d_ssm = ssm_d.shape[1]
    d_conv = conv_w.shape[2]
    n_heads = 4
    t = batch * seq
    assert seq % SSM_CHUNK == 0
    n_chunks = seq // SSM_CHUNK

    h = x.reshape(t, d_model)
    memf = mem.reshape(batch * n_mem, d_model)
    for l in range(depth):
        last = l == depth - 1
        (h, un), (mix_w_in_b,) = _ffn(h, ffn1_norm[l], ffn1_w_in[l], ffn1_w_out[l], mix_norm[l], mode="next_norm",
                                      name="ffn1", side=(mix_w_in[l],))

        (proj,), (ffn2_w_in_b, ffn2_w_out_b, glu_w_b, conv_w_out_b, mix_w_out_b) = _mm(
            un, mix_w_in_b, name="mix_in",
            side=(ffn2_w_in[l], ffn2_w_out[l], ssm_glu_w[l], conv_w_out[l], mix_w_out[l]))

        ys = _ssm(proj, ssm_a_re[l], ssm_a_im[l], ssm_log_dt[l], ssm_b_re[l], ssm_b_im[l],
                  ssm_c_re[l], ssm_c_im[l], ssm_d[l], n_chunks=n_chunks, name="ssm")

        cz = _conv(proj, conv_w[l], batch=batch, seq=seq, d_conv=d_conv, col0=d_ssm, name="conv")

        (merged,), (wq_b, wk_b, wv_b, wo_b) = _merge(
            ys, cz, glu_w_b, conv_w_out_b, proj, d_model=d_model, gate_col0=d_ssm + 3 * d_conv, name="merge",
            side=(xattn_wq[l], xattn_wk[l], xattn_wv[l], xattn_wo[l]))
        h, h_b = _mm_res(merged, mix_w_out_b, h, name="mix_out", bf16_copy=True)

        k, v = _norm_mm(memf, mem_norm[l], (wk_b, wv_b), name="xattn_kv", bn=512)
        o = _q_attn(h_b, xattn_norm[l], wq_b, k, v, batch=batch, seq=seq, n_mem=n_mem, n_heads=n_heads,
                    name="xattn")
        (h,) = _mm_res(o, wo_b, h, name="xattn_o")

        h = _ffn_big(h, ffn2_norm[l], ffn2_w_in_b, ffn2_w_out_b, final_norm,
                     mode="final_norm" if last else "plain", name="ffn2")
    if depth == 0:
        raise NotImplementedError("depth 0")
    return h.reshape(batch, seq, d_model)
```

```python
import functools

import jax
import jax.numpy as jnp
from jax import lax
from jax.experimental import pallas as pl
from jax.experimental.pallas import tpu as pltpu

F32 = jnp.float32
BF16 = jnp.bfloat16
RMS_EPS = 1e-6

V7X_LANES = 128
V7X_MXU_DIM = 256
V7X_VMEM_BYTES = 64 * 1024 * 1024
VMEM_LIMIT = V7X_VMEM_BYTES - 4 * 1024 * 1024

SSM_CHUNK = 16
SSM_GROUPS_PER_BLOCK = 16
SSM_SUB = 4


def _params(dims):
    return pltpu.CompilerParams(dimension_semantics=dims, vmem_limit_bytes=VMEM_LIMIT)


def _blk(dim, pref):
    b = min(dim, pref)
    assert dim % b == 0, (dim, pref)
    return b


def _rms_rows(x, g):
    ms = jnp.mean(x * x, axis=-1, keepdims=True)
    return x * lax.rsqrt(ms + RMS_EPS) * g


def _row_chunks(n_rows, chunk, body):
    chunk = min(chunk, n_rows)
    assert n_rows % chunk == 0

    def step(c, carry):
        body(pl.ds(pl.multiple_of(c * chunk, chunk), chunk))
        return carry

    lax.fori_loop(0, n_rows // chunk, step, 0)


BF16_SUBLANES = 16


def _side_spec(shape, grid):
    r, c = shape
    n_steps = grid[0] * grid[1]
    best = None
    for rb in range(BF16_SUBLANES, r + 1, BF16_SUBLANES):
        if r % rb:
            continue
        for cb in range(V7X_LANES, c + 1, V7X_LANES):
            if c % cb == 0 and (r // rb) * (c // cb) <= n_steps and (best is None or rb * cb < best[0] * best[1]):
                best = (rb, cb)
    assert best is not None, (shape, grid)
    rb, cb = best
    ncb = c // cb
    last = (r // rb) * ncb - 1

    def index(i, j):
        blk = jnp.minimum(i * grid[1] + j, last)
        return blk // ncb, blk % ncb

    return pl.BlockSpec((rb, cb), index)


def _call(body, *, grid, in_specs, out_specs, out_shapes, scratch_shapes=(), args, side=(), name):
    n_in, n_out, n_side = len(in_specs), len(out_specs), len(side)

    def kernel(*refs):
        side_in = refs[n_in:n_in + n_side]
        outs = refs[n_in + n_side:n_in + n_side + n_out]
        side_out = refs[n_in + n_side + n_out:n_in + 2 * n_side + n_out]

        def side_cast():
            for src, dst in zip(side_in, side_out):
                dst[...] = src[...].astype(BF16)

        body(*refs[:n_in], *outs, *refs[n_in + 2 * n_side + n_out:], side_cast=side_cast)

    side_specs = [_side_spec(w.shape, grid) for w in side]
    res = pl.pallas_call(
        kernel,
        grid=grid,
        in_specs=list(in_specs) + side_specs,
        out_specs=list(out_specs) + side_specs,
        out_shape=list(out_shapes) + [jax.ShapeDtypeStruct(w.shape, BF16) for w in side],
        scratch_shapes=list(scratch_shapes),
        compiler_params=_params(("arbitrary",) * len(grid)),
        name=name,
    )(*args, *side)
    return tuple(res[:n_out]), tuple(res[n_out:])


def _ffn_kernel(x_ref, g_ref, wa_ref, wb_ref, wout_ref, g2_ref, *rest, mode, side_cast):
    if mode == "next_norm":
        o_ref, hn_ref, xn_ref = rest
    else:
        o_ref, xn_ref = rest
    f = pl.program_id(1)
    bm = x_ref.shape[0]

    @pl.when(f == 0)
    def _():
        def body(rows):
            xn_ref[rows, :] = _rms_rows(x_ref[rows, :], g_ref[...]).astype(BF16)
            o_ref[rows, :] = jnp.zeros((rows.size, o_ref.shape[1]), F32)

        _row_chunks(bm, 64, body)

    xn = xn_ref[...]
    a = jnp.dot(xn, wa_ref[...], preferred_element_type=F32)
    b = jnp.dot(xn, wb_ref[...], preferred_element_type=F32)
    gated = (a * jax.nn.sigmoid(a) * b).astype(BF16)
    side_cast()
    d = o_ref.shape[1]
    bn = min(d, 1024)
    for n0 in range(0, d, bn):
        o_ref[:, n0:n0 + bn] += jnp.dot(gated, wout_ref[:, n0:n0 + bn], preferred_element_type=F32)

    @pl.when(f == pl.num_programs(1) - 1)
    def _():
        def body(rows):
            h = x_ref[rows, :] + 0.5 * o_ref[rows, :]
            if mode == "final_norm":
                h = _rms_rows(h, g2_ref[...])
            elif mode == "next_norm":
                hn_ref[rows, :] = _rms_rows(h, g2_ref[...]).astype(BF16)
            o_ref[rows, :] = h

        _row_chunks(bm, 64, body)


def _ffn(x, g, w_in, w_out, g2, *, mode, name, side=()):
    t, d = x.shape
    f_dim = w_out.shape[0]
    bf = _blk(f_dim, V7X_MXU_DIM)
    nf = f_dim // bf
    bm = _blk(t, 512)
    w_in = w_in.astype(BF16)
    row_spec = pl.BlockSpec((bm, d), lambda i, f: (i, 0))
    out_specs, out_shapes = [row_spec], [jax.ShapeDtypeStruct((t, d), F32)]
    if mode == "next_norm":
        out_specs, out_shapes = out_specs + [row_spec], out_shapes + [jax.ShapeDtypeStruct((t, d), BF16)]
    return _call(
        functools.partial(_ffn_kernel, mode=mode),
        grid=(t // bm, nf),
        in_specs=[
            row_spec,
            pl.BlockSpec((1, d), lambda i, f: (0, 0)),
            pl.BlockSpec((d, bf), lambda i, f: (0, f)),
            pl.BlockSpec((d, bf), lambda i, f: (0, nf + f)),
            pl.BlockSpec((bf, d), lambda i, f: (f, 0)),
            pl.BlockSpec((1, d), lambda i, f: (0, 0)),
        ],
        out_specs=out_specs,
        out_shapes=out_shapes,
        scratch_shapes=[pltpu.VMEM((bm, d), BF16)],
        args=(x, g.reshape(1, d), w_in, w_in, w_out.astype(BF16), g2.reshape(1, d)),
        side=side,
        name=name,
    )


def _ffn_big_kernel(g_ref, g2_ref, x_hbm, win_hbm, wout_hbm, o_hbm, *rest, mode, nf, bf):
    if mode == "next_norm":
        hn_hbm, *rest = rest
    acc_ref, stage_ref, xn_ref, wa_buf, wb_buf, wo_buf, wsem, iosem = rest
    i = pl.program_id(0)
    n_blocks = pl.num_programs(0)
    bm, d = acc_ref.shape

    def rows_of(blk):
        return pl.ds(pl.multiple_of(blk * bm, bm), bm)

    def x_copy(blk):
        return pltpu.make_async_copy(x_hbm.at[rows_of(blk), :], stage_ref, iosem.at[0])

    def o_copy(blk):
        return pltpu.make_async_copy(acc_ref, o_hbm.at[rows_of(blk), :], iosem.at[1])

    def hn_copy(blk):
        return pltpu.make_async_copy(xn_ref, hn_hbm.at[rows_of(blk), :], iosem.at[2])

    def w_copies(f, slot):
        return (
            pltpu.make_async_copy(win_hbm.at[:, pl.ds(f * bf, bf)], wa_buf.at[slot], wsem.at[0, slot]),
            pltpu.make_async_copy(win_hbm.at[:, pl.ds((nf + f) * bf, bf)], wb_buf.at[slot], wsem.at[1, slot]),
            pltpu.make_async_copy(wout_hbm.at[pl.ds(f * bf, bf), :], wo_buf.at[slot], wsem.at[2, slot]),
        )

    def start(f, slot):
        for c in w_copies(f, slot):
            c.start()

    def wait(f, slot):
        for c in w_copies(f, slot):
            c.wait()

    start(0, 0)

    @pl.when(i == 0)
    def _():
        x_copy(0).start()

    x_copy(i).wait()

    if mode == "next_norm":
        @pl.when(i > 0)
        def _():
            hn_copy(i - 1).wait()

    def normalise(rows):
        xn_ref[rows, :] = _rms_rows(stage_ref[rows, :], g_ref[...]).astype(BF16)

    _row_chunks(bm, 64, normalise)

    @pl.when(i > 0)
    def _():
        o_copy(i - 1).wait()

    def init(rows):
        acc_ref[rows, :] = 2.0 * stage_ref[rows, :]

    _row_chunks(bm, 64, init)

    @pl.when(i + 1 < n_blocks)
    def _():
        x_copy(i + 1).start()

    def tile(slot):
        xn = xn_ref[...]
        a = jnp.dot(xn, wa_buf[slot], preferred_element_type=F32)
        b = jnp.dot(xn, wb_buf[slot], preferred_element_type=F32)
        gated = (a * jax.nn.sigmoid(a) * b).astype(BF16)
        bn = min(d, 2 * V7X_MXU_DIM)
        for n0 in range(0, d, bn):
            acc_ref[:, n0:n0 + bn] += jnp.dot(gated, wo_buf[slot, :, n0:n0 + bn], preferred_element_type=F32)

    n_pairs = (nf - 1) // 2

    def pair(p, carry):
        f0 = 2 * p
        start(f0 + 1, 1)
        wait(f0, 0)
        tile(0)
        start(f0 + 2, 0)
        wait(f0 + 1, 1)
        tile(1)
        return carry

    lax.fori_loop(0, n_pairs, pair, 0)
    f_last = 2 * n_pairs
    if nf - f_last == 2:
        start(f_last + 1, 1)
    wait(f_last, 0)
    tile(0)
    if nf - f_last == 2:
        wait(f_last + 1, 1)
        tile(1)

    def finish(rows):
        h = 0.5 * acc_ref[rows, :]
        if mode == "final_norm":
            h = _rms_rows(h, g2_ref[...])
        elif mode == "next_norm":
            xn_ref[rows, :] = _rms_rows(h, g2_ref[...]).astype(BF16)
        acc_ref[rows, :] = h

    _row_chunks(bm, 64, finish)
    o_copy(i).start()
    if mode == "next_norm":
        hn_copy(i).start()

    @pl.when(i == n_blocks - 1)
    def _():
        o_copy(i).wait()
        if mode == "next_norm":
            hn_copy(i).wait()


def _ffn_big(x, g, w_in, w_out, g2, *, mode, name):
    t, d = x.shape
    f_dim = w_out.shape[0]
    bf = _blk(f_dim, V7X_MXU_DIM)
    nf = f_dim // bf
    bm = _blk(t, 1024)
    vec_spec = pl.BlockSpec((1, d), lambda i: (0, 0))
    hbm = pl.BlockSpec(memory_space=pl.ANY)
    two = mode == "next_norm"
    return pl.pallas_call(
        functools.partial(_ffn_big_kernel, mode=mode, nf=nf, bf=bf),
        grid=(t // bm,),
        in_specs=[vec_spec, vec_spec, hbm, hbm, hbm],
        out_specs=[hbm] + [hbm] * two,
        out_shape=[jax.ShapeDtypeStruct((t, d), F32)] + [jax.ShapeDtypeStruct((t, d), BF16)] * two,
        scratch_shapes=[
            pltpu.VMEM((bm, d), F32),
            pltpu.VMEM((bm, d), F32),
            pltpu.VMEM((bm, d), BF16),
            pltpu.VMEM((2, d, bf), BF16),
            pltpu.VMEM((2, d, bf), BF16),
            pltpu.VMEM((2, bf, d), BF16),
            pltpu.SemaphoreType.DMA((3, 2)),
            pltpu.SemaphoreType.DMA((3,)),
        ],
        compiler_params=_params(("arbitrary",)),
        name=name,
    )(g.reshape(1, d), g2.reshape(1, d), x, w_in.astype(BF16), w_out.astype(BF16))


def _ffn_loop_kernel(x_ref, g_ref, g2_ref, win_hbm, wout_hbm, o_ref, xn_ref, wa_buf, wb_buf, wo_buf, sem,
                     *, mode, nf, bf):
    bm, d = x_ref.shape

    def copies_in(f, slot):
        return (
            pltpu.make_async_copy(win_hbm.at[:, pl.ds(f * bf, bf)], wa_buf.at[slot], sem.at[0, slot]),
            pltpu.make_async_copy(win_hbm.at[:, pl.ds((nf + f) * bf, bf)], wb_buf.at[slot], sem.at[1, slot]),
        )

    def copies_out(f, slot):
        return (pltpu.make_async_copy(wout_hbm.at[pl.ds(f * bf, bf), :], wo_buf.at[pl.ds(slot * bf, bf), :],
                                      sem.at[2, slot]),)

    def start(copies):
        for c in copies:
            c.start()

    def wait(copies):
        for c in copies:
            c.wait()

    start(copies_in(0, 0))
    start(copies_out(0, 0))

    def prologue(rows):
        xn_ref[rows, :] = _rms_rows(x_ref[rows, :], g_ref[...]).astype(BF16)
        o_ref[rows, :] = jnp.zeros((rows.size, d), F32)

    _row_chunks(bm, 64, prologue)

    def gate(slot):
        xn = xn_ref[...]
        a = jnp.dot(xn, wa_buf[slot], preferred_element_type=F32)
        b = jnp.dot(xn, wb_buf[slot], preferred_element_type=F32)
        return (a * jax.nn.sigmoid(a) * b).astype(BF16)

    def project(gated, w_rows):
        bn = min(d, 1024)
        for n0 in range(0, d, bn):
            o_ref[:, n0:n0 + bn] += jnp.dot(gated, wo_buf[w_rows, n0:n0 + bn], preferred_element_type=F32)

    def two_tiles(f0, more):
        start(copies_in(f0 + 1, 1))
        start(copies_out(f0 + 1, 1))
        wait(copies_in(f0, 0))
        g0 = gate(0)
        if more:
            start(copies_in(f0 + 2, 0))
        wait(copies_in(f0 + 1, 1))
        g1 = gate(1)
        wait(copies_out(f0, 0))
        wait(copies_out(f0 + 1, 1))
        project(jnp.concatenate([g0, g1], axis=1), slice(0, 2 * bf))
        if more:
            start(copies_out(f0 + 2, 0))

    n_pairs = (nf - 1) // 2

    def pair(p, carry):
        two_tiles(2 * p, True)
        return carry

    lax.fori_loop(0, n_pairs, pair, 0)
    f_last = 2 * n_pairs
    if nf - f_last == 2:
        two_tiles(f_last, False)
    else:
        wait(copies_in(f_last, 0))
        g0 = gate(0)
        wait(copies_out(f_last, 0))
        project(g0, slice(0, bf))

    def epilogue(rows):
        h = x_ref[rows, :] + 0.5 * o_ref[rows, :]
        if mode == "final_norm":
            h = _rms_rows(h, g2_ref[...])
        o_ref[rows, :] = h

    _row_chunks(bm, 64, epilogue)


def _ffn_loop(x, g, w_in, w_out, g2, *, mode, name):
    t, d = x.shape
    f_dim = w_out.shape[0]
    bf = _blk(f_dim, V7X_MXU_DIM)
    nf = f_dim // bf
    bm = _blk(t, 512)
    row_spec = pl.BlockSpec((bm, d), lambda i: (i, 0))
    vec_spec = pl.BlockSpec((1, d), lambda i: (0, 0))
    return pl.pallas_call(
        functools.partial(_ffn_loop_kernel, mode=mode, nf=nf, bf=bf),
        grid=(t // bm,),
        in_specs=[row_spec, vec_spec, vec_spec,
                  pl.BlockSpec(memory_space=pl.ANY), pl.BlockSpec(memory_space=pl.ANY)],
        out_specs=row_spec,
        out_shape=jax.ShapeDtypeStruct((t, d), F32),
        scratch_shapes=[
            pltpu.VMEM((bm, d), BF16),
            pltpu.VMEM((2, d, bf), BF16),
            pltpu.VMEM((2, d, bf), BF16),
            pltpu.VMEM((2 * bf, d), BF16),
            pltpu.SemaphoreType.DMA((3, 2)),
        ],
        compiler_params=_params(("arbitrary",)),
        name=name,
    )(x, g.reshape(1, d), g2.reshape(1, d), w_in.astype(BF16), w_out.astype(BF16))


def _mm_kernel(a_ref, w_ref, o_ref, *, side_cast):
    o_ref[...] = jnp.dot(a_ref[...], w_ref[...], preferred_element_type=F32).astype(o_ref.dtype)
    side_cast()


def _mm(a, w, *, name, bm=1024, bn=1024, side=()):
    m, k = a.shape
    n = w.shape[1]
    bm = _blk(m, bm)
    bn = _blk(n, bn)
    return _call(
        _mm_kernel,
        grid=(m // bm, n // bn),
        in_specs=[
            pl.BlockSpec((bm, k), lambda i, j: (i, 0)),
            pl.BlockSpec((k, bn), lambda i, j: (0, j)),
        ],
        out_specs=[pl.BlockSpec((bm, bn), lambda i, j: (i, j))],
        out_shapes=[jax.ShapeDtypeStruct((m, n), BF16)],
        args=(a, w.astype(BF16)),
        side=side,
        name=name,
    )


def _norm_mm_kernel(a_ref, g_ref, *rest, n_w):
    w_refs, o_refs, an_ref = rest[:n_w], rest[n_w:2 * n_w], rest[2 * n_w]

    @pl.when(pl.program_id(1) == 0)
    def _():
        def body(rows):
            an_ref[rows, :] = _rms_rows(a_ref[rows, :].astype(F32), g_ref[...]).astype(BF16)

        _row_chunks(a_ref.shape[0], 64, body)

    for w_ref, o_ref in zip(w_refs, o_refs):
        o_ref[...] = jnp.dot(an_ref[...], w_ref[...], preferred_element_type=F32).astype(o_ref.dtype)


def _norm_mm(a, g, ws, *, name, bm=512, bn=1024):
    m, k = a.shape
    n = ws[0].shape[1]
    bm = _blk(m, bm)
    bn = _blk(n, bn)
    tile = pl.BlockSpec((bm, bn), lambda i, j: (i, j))
    return pl.pallas_call(
        functools.partial(_norm_mm_kernel, n_w=len(ws)),
        grid=(m // bm, n // bn),
        in_specs=[
            pl.BlockSpec((bm, k), lambda i, j: (i, 0)),
            pl.BlockSpec((1, k), lambda i, j: (0, 0)),
        ] + [pl.BlockSpec((k, bn), lambda i, j: (0, j))] * len(ws),
        out_specs=[tile] * len(ws),
        out_shape=[jax.ShapeDtypeStruct((m, n), BF16)] * len(ws),
        scratch_shapes=[pltpu.VMEM((bm, k), BF16)],
        compiler_params=_params(("arbitrary", "arbitrary")),
        name=name,
    )(a, g.reshape(1, k), *[w.astype(BF16) for w in ws])


def _mm_res_kernel(a_ref, w_ref, res_ref, o_ref, *copy_ref):
    bn = o_ref.shape[1]
    cn = min(bn, 2 * V7X_MXU_DIM)
    for n0 in range(0, bn, cn):
        cols = slice(n0, n0 + cn)
        out = res_ref[:, cols] + jnp.dot(a_ref[...], w_ref[:, cols], preferred_element_type=F32)
        o_ref[:, cols] = out
        for ref in copy_ref:
            ref[:, cols] = out.astype(ref.dtype)


def _mm_res(a, w, res, *, name, bm=1024, bn=1024, bf16_copy=False):
    m, k = a.shape
    n = w.shape[1]
    bm = _blk(m, bm)
    bn = _blk(n, bn)
    tile = pl.BlockSpec((bm, bn), lambda i, j: (i, j))
    return pl.pallas_call(
        _mm_res_kernel,
        grid=(m // bm, n // bn),
        in_specs=[
            pl.BlockSpec((bm, k), lambda i, j: (i, 0)),
            pl.BlockSpec((k, bn), lambda i, j: (0, j)),
            tile,
        ],
        out_specs=[tile] + [tile] * bf16_copy,
        out_shape=[jax.ShapeDtypeStruct((m, n), F32)] + [jax.ShapeDtypeStruct((m, n), BF16)] * bf16_copy,
        compiler_params=_params(("parallel", "arbitrary")),
        name=name,
    )(a, w.astype(BF16), res)


def _cmul(ar, ai, br, bi):
    return ar * br - ai * bi, ar * bi + ai * br


def _gelu_exact(x):
    return 0.5 * x * (1.0 + lax.erf(x * (2.0 ** -0.5)))


def _ssm_kernel(tok_ref, are_ref, aim_ref, ldt_ref, braw_ref, craw_ref, dsk_ref, o_ref,
                bstack_ref, nst_ref, dstack_ref, pw_ref, apow_ref, sre_ref, sim_ref, w_ref,
                slab_ref, u_ref, *, n_seq, n_chunks, n_log):
    w256 = V7X_MXU_DIM
    ns = braw_ref.shape[2] // 2
    n_strip = ns // V7X_LANES
    off = sre_ref.shape[2] - n_chunks
    rows = n_seq * n_chunks
    n_slab = w256 // V7X_LANES

    for hf in range(n_slab):
        slab_ref[hf] = tok_ref[:, hf * V7X_LANES:(hf + 1) * V7X_LANES].astype(F32)
    for r in range(SSM_CHUNK):
        parts = [slab_ref[hf, pl.ds(r, rows, stride=SSM_CHUNK), :] for hf in range(n_slab)]
        u_ref[r] = jnp.concatenate(parts, axis=1).astype(BF16)

    @pl.when(pl.program_id(1) == 0)
    def _prepare():
        ar = are_ref[0]
        ai = aim_ref[0]
        dt = jnp.exp(ldt_ref[0])
        mag = jnp.exp(ar * dt)
        lr = mag * jnp.cos(ai * dt)
        li = mag * jnp.sin(ai * dt)
        den = ar * ar + ai * ai
        zr = ((lr - 1.0) * ar + li * ai) / den
        zi = (li * ar - (lr - 1.0) * ai) / den
        sr, si = _cmul(braw_ref[0, :, :ns], braw_ref[0, :, ns:], zr, zi)
        cr = craw_ref[0, :, :ns]
        ci = craw_ref[0, :, ns:]
        cn = jnp.concatenate([cr, -ci], axis=1).astype(BF16)
        for j in range(SSM_CHUNK):
            bs = jnp.concatenate([sr, si], axis=1).astype(BF16)
            dj = lax.dot_general(bs, cn, (((1,), (1,)), ((), ())), preferred_element_type=F32)
            dstack_ref[(SSM_CHUNK - 1 - j) * w256:(SSM_CHUNK - j) * w256, :] = dj.astype(BF16)
            if j < SSM_SUB:
                bstack_ref[(SSM_SUB - 1 - j) * w256:(SSM_SUB - j) * w256, :] = bs
            sr, si = _cmul(sr, si, lr, li)
        mr, mi = lr, li
        for s in range(SSM_SUB):
            er, ei = _cmul(cr, ci, mr, mi)
            nst_ref[s * w256:(s + 1) * w256, :] = jnp.concatenate([er, -ei], axis=1).astype(BF16)
            if s < SSM_SUB - 1:
                mr, mi = _cmul(mr, mi, lr, li)
        pw_ref[0:1, :] = mr
        pw_ref[1:2, :] = mi
        p8r, p8i = _cmul(mr, mi, mr, mi)
        kr, ki = _cmul(p8r, p8i, p8r, p8i)
        for k in range(n_log):
            for s in range(n_strip):
                apow_ref[k, s, 0:1, :] = kr[:, s * V7X_LANES:(s + 1) * V7X_LANES]
                apow_ref[k, s, 1:2, :] = ki[:, s * V7X_LANES:(s + 1) * V7X_LANES]
            kr, ki = _cmul(kr, ki, kr, ki)
        zeros = jnp.zeros((off, V7X_LANES), F32)
        for s in range(n_strip):
            for q in range(n_seq):
                sre_ref[s, q, 0:off, :] = zeros
                sim_ref[s, q, 0:off, :] = zeros

    p4r = pw_ref[0:1, :]
    p4i = pw_ref[1:2, :]

    xr = xi = None
    for q in range(SSM_SUB):
        z = None
        for s in range(SSM_SUB):
            d = jnp.dot(u_ref[SSM_SUB * q + s], bstack_ref[s * w256:(s + 1) * w256, :],
                        preferred_element_type=F32)
            z = d if z is None else z + d
        if xr is None:
            xr, xi = z[:, :ns], z[:, ns:]
        else:
            xr, xi = _cmul(xr, xi, p4r, p4i)
            xr, xi = xr + z[:, :ns], xi + z[:, ns:]

    for s in range(n_strip):
        lanes = slice(s * V7X_LANES, (s + 1) * V7X_LANES)
        for q in range(n_seq):
            sre_ref[s, q, off:, :] = xr[q * n_chunks:(q + 1) * n_chunks, lanes]
            sim_ref[s, q, off:, :] = xi[q * n_chunks:(q + 1) * n_chunks, lanes]
    for s in range(n_strip):
        for q in range(n_seq):
            for k in range(n_log):
                sh = 1 << k
                kr = apow_ref[k, s, 0:1, :]
                ki = apow_ref[k, s, 1:2, :]
                pr = sre_ref[s, q, off - sh:off - sh + n_chunks, :]
                pi = sim_ref[s, q, off - sh:off - sh + n_chunks, :]
                tr, ti = _cmul(pr, pi, kr, ki)
                sre_ref[s, q, off:, :] = sre_ref[s, q, off:, :] + tr
                sim_ref[s, q, off:, :] = sim_ref[s, q, off:, :] + ti

    ysts = []
    for q in range(SSM_SUB):
        for s in range(n_strip):
            lanes_r = slice(s * V7X_LANES, (s + 1) * V7X_LANES)
            lanes_i = slice(ns + s * V7X_LANES, ns + (s + 1) * V7X_LANES)
            for sq in range(n_seq):
                rs = slice(sq * n_chunks, (sq + 1) * n_chunks)
                if q == 0:
                    vr = sre_ref[s, sq, off - 1:off - 1 + n_chunks, :]
                    vi = sim_ref[s, sq, off - 1:off - 1 + n_chunks, :]
                else:
                    vr, vi = _cmul(sre_ref[s, sq, off:, :], sim_ref[s, sq, off:, :],
                                   p4r[:, lanes_r], p4i[:, lanes_r])
                if q < SSM_SUB - 1:
                    sre_ref[s, sq, off:, :] = vr
                    sim_ref[s, sq, off:, :] = vi
                w_ref[rs, lanes_r] = vr.astype(BF16)
                w_ref[rs, lanes_i] = vi.astype(BF16)
        ysts.append(lax.dot_general(w_ref[...], nst_ref[...], (((1,), (1,)), ((), ())),
                                    preferred_element_type=F32))

    dsk = dsk_ref[0]
    for t in range(SSM_CHUNK):
        y = ysts[t // SSM_SUB][:, (t % SSM_SUB) * w256:(t % SSM_SUB + 1) * w256]
        y = y + dsk * u_ref[t].astype(F32)
        for r in range(t + 1):
            blk = SSM_CHUNK - 1 - t + r
            y = y + jnp.dot(u_ref[r], dstack_ref[blk * w256:(blk + 1) * w256, :],
                            preferred_element_type=F32)
        y = _gelu_exact(y)
        for hf in range(n_slab):
            slab_ref[hf, pl.ds(t, rows, stride=SSM_CHUNK), :] = y[:, hf * V7X_LANES:(hf + 1) * V7X_LANES]
    for hf in range(n_slab):
        o_ref[:, hf * V7X_LANES:(hf + 1) * V7X_LANES] = slab_ref[hf].astype(o_ref.dtype)


def _ssm(proj, a_re, a_im, log_dt, b_re, b_im, c_re, c_im, d_skip, *, n_chunks, name):
    n_tok = proj.shape[0]
    d_ssm = d_skip.shape[0]
    n_rows = n_tok // SSM_CHUNK
    n_groups, n_state = a_re.shape
    h = d_ssm // n_groups
    gpb = SSM_GROUPS_PER_BLOCK
    assert gpb * h == V7X_MXU_DIM and n_groups % gpb == 0
    n_blocks = n_groups // gpb
    ns = gpb * n_state
    n_seq = 1
    n_log = max(1, (n_chunks - 1).bit_length())
    off = max(8, 1 << (n_log - 1))

    col_group = (jnp.arange(2 * ns) % ns) // n_state
    same_group = (jnp.arange(gpb)[:, None] == col_group[None, :]).astype(F32)

    def block_diag(x_re, x_im):
        lanes = lambda x: x.reshape(n_blocks, gpb, h, n_state).transpose(0, 2, 1, 3).reshape(n_blocks, h, ns)
        x = jnp.concatenate([lanes(x_re), lanes(x_im)], axis=2)
        x = x[:, None, :, :] * same_group[None, :, None, :]
        return x.reshape(n_blocks, gpb * h, 2 * ns)

    braw = block_diag(b_re.transpose(0, 2, 1), b_im.transpose(0, 2, 1))
    craw = block_diag(c_re, c_im)
    vec = lambda x: x.reshape(n_blocks, 1, ns)
    ldt = jnp.broadcast_to(log_dt[:, None], (n_groups, n_state))
    rows = n_seq * n_chunks
    kernel = functools.partial(_ssm_kernel, n_seq=n_seq, n_chunks=n_chunks, n_log=n_log)
    vspec = pl.BlockSpec((1, 1, ns), lambda g, r: (g, 0, 0))
    mspec = pl.BlockSpec((1, V7X_MXU_DIM, 2 * ns), lambda g, r: (g, 0, 0))
    uspec = pl.BlockSpec((rows * SSM_CHUNK, V7X_MXU_DIM), lambda g, r: (r, g))
    return pl.pallas_call(
        kernel,
        grid=(n_blocks, n_rows // rows),
        in_specs=[uspec, vspec, vspec, vspec, mspec, mspec,
                  pl.BlockSpec((1, 1, V7X_MXU_DIM), lambda g, r: (g, 0, 0))],
        out_specs=uspec,
        out_shape=jax.ShapeDtypeStruct((n_tok, d_ssm), BF16),
        scratch_shapes=[
            pltpu.VMEM((SSM_SUB * V7X_MXU_DIM, 2 * ns), BF16),
            pltpu.VMEM((SSM_SUB * V7X_MXU_DIM, 2 * ns), BF16),
            pltpu.VMEM((SSM_CHUNK * V7X_MXU_DIM, V7X_MXU_DIM), BF16),
            pltpu.VMEM((8, ns), F32),
            pltpu.VMEM((n_log, ns // V7X_LANES, 8, V7X_LANES), F32),
            pltpu.VMEM((ns // V7X_LANES, n_seq, off + n_chunks, V7X_LANES), F32),
            pltpu.VMEM((ns // V7X_LANES, n_seq, off + n_chunks, V7X_LANES), F32),
            pltpu.VMEM((rows, 2 * ns), BF16),
            pltpu.VMEM((V7X_MXU_DIM // V7X_LANES, rows * SSM_CHUNK, V7X_LANES), F32),
            pltpu.VMEM((SSM_CHUNK, rows, V7X_MXU_DIM), BF16),
        ],
        compiler_params=_params(("parallel", "arbitrary")),
        name=name,
    )(proj, vec(a_re), vec(a_im), vec(ldt), braw, craw, d_skip.reshape(n_blocks, 1, V7X_MXU_DIM))


def _conv_kernel(cb_ref, cc_ref, ch_ref, w_ref, o_ref, z_ref, *, chunk):
    seq = cb_ref.shape[0]
    pad = z_ref.shape[0] - seq
    kw = w_ref.shape[0]
    z_ref[0:pad, :] = jnp.zeros((pad, z_ref.shape[1]), F32)

    def fill(rows):
        z_ref[pl.ds(rows.start + pad, rows.size), :] = cc_ref[rows, :].astype(F32) * ch_ref[rows, :].astype(F32)

    _row_chunks(seq, chunk, fill)
    for c in range(seq // chunk):
        r0 = c * chunk
        acc = None
        for k in range(kw):
            lag = kw - 1 - k
            term = w_ref[k:k + 1, :] * z_ref[pad + r0 - lag:pad + r0 - lag + chunk, :]
            acc = term if acc is None else acc + term
        o_ref[r0:r0 + chunk, :] = (cb_ref[r0:r0 + chunk, :].astype(F32) * acc).astype(o_ref.dtype)


def _conv(proj, conv_w, *, batch, seq, d_conv, col0, name):
    bc = _blk(d_conv, V7X_MXU_DIM)
    nb = d_conv // bc
    chunk = min(seq, 512)
    kw = conv_w.shape[0]
    cspec = lambda k: pl.BlockSpec((seq, bc), lambda b, j: (b, (col0 + k * d_conv) // bc + j))
    return pl.pallas_call(
        functools.partial(_conv_kernel, chunk=chunk),
        grid=(batch, nb),
        in_specs=[cspec(0), cspec(1), cspec(2), pl.BlockSpec((kw, bc), lambda b, j: (0, j))],
        out_specs=pl.BlockSpec((seq, bc), lambda b, j: (b, j)),
        out_shape=jax.ShapeDtypeStruct((batch * seq, d_conv), BF16),
        scratch_shapes=[pltpu.VMEM((8 + seq, bc), F32)],
        compiler_params=_params(("parallel", "parallel")),
        name=name,
    )(proj, proj, proj, conv_w)


def _merge_kernel(ys_ref, cz_ref, wv_ref, wg_ref, wc_ref, ga_ref, gb_ref, o_ref, *, side_cast):
    side_cast()
    ys = ys_ref[...]
    val = jnp.dot(ys, wv_ref[...], preferred_element_type=F32)
    gate = jnp.dot(ys, wg_ref[...], preferred_element_type=F32)
    yb = jnp.dot(cz_ref[...], wc_ref[...], preferred_element_type=F32)
    ya = val * jax.nn.sigmoid(gate)
    out = jax.nn.sigmoid(ga_ref[...].astype(F32)) * ya + jax.nn.sigmoid(gb_ref[...].astype(F32)) * yb
    o_ref[...] = out.astype(o_ref.dtype)


def _merge(ys, cz, glu_w, conv_w_out, proj, *, d_model, gate_col0, name, bm=1024, bn=512, side=()):
    t, k = ys.shape
    bm = _blk(t, bm)
    bn = _blk(d_model, bn)
    nj = d_model // bn
    aspec = pl.BlockSpec((bm, k), lambda i, j: (i, 0))
    wspec = lambda off: pl.BlockSpec((k, bn), lambda i, j: (0, off + j))
    gspec = lambda off: pl.BlockSpec((bm, bn), lambda i, j: (i, off + j))
    glu_w = glu_w.astype(BF16)
    return _call(
        _merge_kernel,
        grid=(t // bm, nj),
        in_specs=[aspec, aspec, wspec(0), wspec(nj), wspec(0),
                  gspec(gate_col0 // bn), gspec(gate_col0 // bn + nj)],
        out_specs=[pl.BlockSpec((bm, bn), lambda i, j: (i, j))],
        out_shapes=[jax.ShapeDtypeStruct((t, d_model), BF16)],
        args=(ys, cz, glu_w, glu_w, conv_w_out.astype(BF16), proj, proj),
        side=side,
        name=name,
    )


def _attend(q, k, v, scale):
    s = lax.dot_general(q.astype(BF16), k, (((1,), (1,)), ((), ())), preferred_element_type=F32) * scale
    p = jnp.exp(s - jnp.max(s, axis=-1, keepdims=True))
    denom = jnp.sum(p, axis=-1, keepdims=True)
    return jnp.dot(p.astype(BF16), v, preferred_element_type=F32) / denom


def _q_attn_kernel(a_ref, g_ref, wq_ref, k_ref, v_ref, o_ref, an_ref, *, scale, row_chains):
    @pl.when(pl.program_id(1) == 0)
    def _():
        def body(rows):
            an_ref[rows, :] = _rms_rows(a_ref[rows, :].astype(F32), g_ref[...]).astype(BF16)

        _row_chunks(a_ref.shape[0], 64, body)

    rc = a_ref.shape[0] // row_chains
    for r0 in range(0, a_ref.shape[0], rc):
        q = jnp.dot(an_ref[r0:r0 + rc, :], wq_ref[...], preferred_element_type=F32)
        o_ref[r0:r0 + rc, :] = _attend(q, k_ref[...], v_ref[...], scale).astype(o_ref.dtype)


def _q_attn(a, g, wq, k, v, *, batch, seq, n_mem, n_heads, name, bm=1024, row_chains=2):
    t, d = a.shape
    dh = d // n_heads
    bm = _blk(seq, bm)
    per_seq = seq // bm
    return pl.pallas_call(
        functools.partial(_q_attn_kernel, scale=dh ** -0.5, row_chains=row_chains),
        grid=(t // bm, n_heads),
        in_specs=[
            pl.BlockSpec((bm, d), lambda i, h: (i, 0)),
            pl.BlockSpec((1, d), lambda i, h: (0, 0)),
            pl.BlockSpec((d, dh), lambda i, h: (0, h)),
            pl.BlockSpec((n_mem, dh), lambda i, h: (i // per_seq, h)),
            pl.BlockSpec((n_mem, dh), lambda i, h: (i // per_seq, h)),
        ],
        out_specs=pl.BlockSpec((bm, dh), lambda i, h: (i, h)),
        out_shape=jax.ShapeDtypeStruct((t, d), BF16),
        scratch_shapes=[pltpu.VMEM((bm, d), BF16)],
        compiler_params=_params(("arbitrary", "arbitrary")),
        name=name,
    )(a, g.reshape(1, d), wq.astype(BF16), k, v)


def kernel(x, mem, ffn1_norm, ffn1_w_in, ffn1_w_out, mix_norm, mix_w_in, ssm_a_re, ssm_a_im, ssm_log_dt, ssm_b_re, ssm_b_im, ssm_c_re, ssm_c_im, ssm_d, ssm_glu_w, conv_w, conv_w_out, mix_w_out, xattn_norm, mem_norm, xattn_wq, xattn_wk, xattn_wv, xattn_wo, ffn2_norm, ffn2_w_in, ffn2_w_out, final_norm):
    batch, seq, d_model = x.shape
    n_mem = mem.shape[1]
    depth = ffn1_norm.shape[0]
    d_ssm = ssm_d.shape[1]
    d_conv = conv_w.shape[2]
    n_heads = 4
    t = batch * seq
    assert seq % SSM_CHUNK == 0
    n_chunks = seq // SSM_CHUNK

    h = x.reshape(t, d_model)
    memf = mem.reshape(batch * n_mem, d_model)
    for l in range(depth):
        last = l == depth - 1
        h, un = _ffn_big(h, ffn1_norm[l], ffn1_w_in[l], ffn1_w_out[l], mix_norm[l], mode="next_norm", name="ffn1")

        (proj,), (ffn2_w_in_b, ffn2_w_out_b, glu_w_b, conv_w_out_b, mix_w_out_b) = _mm(
            un, mix_w_in[l], name="mix_in",
            side=(ffn2_w_in[l], ffn2_w_out[l], ssm_glu_w[l], conv_w_out[l], mix_w_out[l]))

        ys = _ssm(proj, ssm_a_re[l], ssm_a_im[l], ssm_log_dt[l], ssm_b_re[l], ssm_b_im[l],
                  ssm_c_re[l], ssm_c_im[l], ssm_d[l], n_chunks=n_chunks, name="ssm")

        cz = _conv(proj, conv_w[l], batch=batch, seq=seq, d_conv=d_conv, col0=d_ssm, name="conv")

        (merged,), (wq_b, wk_b, wv_b, wo_b) = _merge(
            ys, cz, glu_w_b, conv_w_out_b, proj, d_model=d_model, gate_col0=d_ssm + 3 * d_conv, name="merge",
            side=(xattn_wq[l], xattn_wk[l], xattn_wv[l], xattn_wo[l]))
        h, h_b = _mm_res(merged, mix_w_out_b, h, name="mix_out", bf16_copy=True)

        k, v = _norm_mm(memf, mem_norm[l], (wk_b, wv_b), name="xattn_kv", bn=512)
        o = _q_attn(h_b, xattn_norm[l], wq_b, k, v, batch=batch, seq=seq, n_mem=n_mem, n_heads=n_heads,
                    name="xattn")
        (h,) = _mm_res(o, wo_b, h, name="xattn_o")

        (h,) = _ffn_big(h, ffn2_norm[l], ffn2_w_in_b, ffn2_w_out_b, final_norm,
                        mode="final_norm" if last else "plain", name="ffn2")
    if depth == 0:
        raise NotImplementedError("depth 0")
    return h.reshape(batch, seq, d_model)
```

```python
import functools

import jax
import jax.numpy as jnp
from jax import lax
from jax.experimental import pallas as pl
from jax.experimental.pallas import tpu as pltpu

F32 = jnp.float32
BF16 = jnp.bfloat16
RMS_EPS = 1e-6

V7X_LANES = 128
V7X_MXU_DIM = 256
V7X_VMEM_BYTES = 64 * 1024 * 1024
VMEM_LIMIT = V7X_VMEM_BYTES - 4 * 1024 * 1024

SSM_CHUNK = 16
SSM_GROUPS_PER_BLOCK = 16
SSM_SUB = 4


def _params(dims):
    return pltpu.CompilerParams(dimension_semantics=dims, vmem_limit_bytes=VMEM_LIMIT)


def _blk(dim, pref):
    b = min(dim, pref)
    assert dim % b == 0, (dim, pref)
    return b


def _rms_rows(x, g):
    ms = jnp.mean(x * x, axis=-1, keepdims=True)
    return x * lax.rsqrt(ms + RMS_EPS) * g


def _row_chunks(n_rows, chunk, body):
    chunk = min(chunk, n_rows)
    assert n_rows % chunk == 0

    def step(c, carry):
        body(pl.ds(pl.multiple_of(c * chunk, chunk), chunk))
        return carry

    lax.fori_loop(0, n_rows // chunk, step, 0)


BF16_SUBLANES = 16


def _side_spec(shape, grid):
    r, c = shape
    n_steps = grid[0] * grid[1]
    best = None
    for rb in range(BF16_SUBLANES, r + 1, BF16_SUBLANES):
        if r % rb:
            continue
        for cb in range(V7X_LANES, c + 1, V7X_LANES):
            if c % cb == 0 and (r // rb) * (c // cb) <= n_steps and (best is None or rb * cb < best[0] * best[1]):
                best = (rb, cb)
    assert best is not None, (shape, grid)
    rb, cb = best
    ncb = c // cb
    last = (r // rb) * ncb - 1

    def index(i, j):
        blk = jnp.minimum(i * grid[1] + j, last)
        return blk // ncb, blk % ncb

    return pl.BlockSpec((rb, cb), index)


def _call(body, *, grid, in_specs, out_specs, out_shapes, scratch_shapes=(), args, side=(), name):
    n_in, n_out, n_side = len(in_specs), len(out_specs), len(side)

    def kernel(*refs):
        side_in = refs[n_in:n_in + n_side]
        outs = refs[n_in + n_side:n_in + n_side + n_out]
        side_out = refs[n_in + n_side + n_out:n_in + 2 * n_side + n_out]

        def side_cast():
            for src, dst in zip(side_in, side_out):
                dst[...] = src[...].astype(BF16)

        body(*refs[:n_in], *outs, *refs[n_in + 2 * n_side + n_out:], side_cast=side_cast)

    side_specs = [_side_spec(w.shape, grid) for w in side]
    res = pl.pallas_call(
        kernel,
        grid=grid,
        in_specs=list(in_specs) + side_specs,
        out_specs=list(out_specs) + side_specs,
        out_shape=list(out_shapes) + [jax.ShapeDtypeStruct(w.shape, BF16) for w in side],
        scratch_shapes=list(scratch_shapes),
        compiler_params=_params(("arbitrary",) * len(grid)),
        name=name,
    )(*args, *side)
    return tuple(res[:n_out]), tuple(res[n_out:])


def _ffn_kernel(x_ref, g_ref, wa_ref, wb_ref, wout_ref, g2_ref, *rest, mode, side_cast):
    if mode == "next_norm":
        o_ref, hn_ref, xn_ref = rest
    else:
        o_ref, xn_ref = rest
    f = pl.program_id(1)
    bm = x_ref.shape[0]

    @pl.when(f == 0)
    def _():
        def body(rows):
            xn_ref[rows, :] = _rms_rows(x_ref[rows, :], g_ref[...]).astype(BF16)
            o_ref[rows, :] = jnp.zeros((rows.size, o_ref.shape[1]), F32)

        _row_chunks(bm, 64, body)

    xn = xn_ref[...]
    a = jnp.dot(xn, wa_ref[...], preferred_element_type=F32)
    b = jnp.dot(xn, wb_ref[...], preferred_element_type=F32)
    gated = (a * jax.nn.sigmoid(a) * b).astype(BF16)
    side_cast()
    d = o_ref.shape[1]
    bn = min(d, 1024)
    for n0 in range(0, d, bn):
        o_ref[:, n0:n0 + bn] += jnp.dot(gated, wout_ref[:, n0:n0 + bn], preferred_element_type=F32)

    @pl.when(f == pl.num_programs(1) - 1)
    def _():
        def body(rows):
            h = x_ref[rows, :] + 0.5 * o_ref[rows, :]
            if mode == "final_norm":
                h = _rms_rows(h, g2_ref[...])
            elif mode == "next_norm":
                hn_ref[rows, :] = _rms_rows(h, g2_ref[...]).astype(BF16)
            o_ref[rows, :] = h

        _row_chunks(bm, 64, body)


def _ffn(x, g, w_in, w_out, g2, *, mode, name, side=()):
    t, d = x.shape
    f_dim = w_out.shape[0]
    bf = _blk(f_dim, V7X_MXU_DIM)
    nf = f_dim // bf
    bm = _blk(t, 512)
    w_in = w_in.astype(BF16)
    row_spec = pl.BlockSpec((bm, d), lambda i, f: (i, 0))
    out_specs, out_shapes = [row_spec], [jax.ShapeDtypeStruct((t, d), F32)]
    if mode == "next_norm":
        out_specs, out_shapes = out_specs + [row_spec], out_shapes + [jax.ShapeDtypeStruct((t, d), BF16)]
    return _call(
        functools.partial(_ffn_kernel, mode=mode),
        grid=(t // bm, nf),
        in_specs=[
            row_spec,
            pl.BlockSpec((1, d), lambda i, f: (0, 0)),
            pl.BlockSpec((d, bf), lambda i, f: (0, f)),
            pl.BlockSpec((d, bf), lambda i, f: (0, nf + f)),
            pl.BlockSpec((bf, d), lambda i, f: (f, 0)),
            pl.BlockSpec((1, d), lambda i, f: (0, 0)),
        ],
        out_specs=out_specs,
        out_shapes=out_shapes,
        scratch_shapes=[pltpu.VMEM((bm, d), BF16)],
        args=(x, g.reshape(1, d), w_in, w_in, w_out.astype(BF16), g2.reshape(1, d)),
        side=side,
        name=name,
    )


def _ffn_big_kernel(g_ref, g2_ref, x_hbm, win_hbm, wout_hbm, o_hbm, *rest, mode, nf, bf):
    if mode == "next_norm":
        hn_hbm, *rest = rest
    acc_ref, stage_ref, xn_ref, wa_buf, wb_buf, wo_buf, wsem, iosem = rest
    i = pl.program_id(0)
    n_blocks = pl.num_programs(0)
    bm, d = acc_ref.shape

    def rows_of(blk):
        return pl.ds(pl.multiple_of(blk * bm, bm), bm)

    def x_copy(blk):
        return pltpu.make_async_copy(x_hbm.at[rows_of(blk), :], stage_ref, iosem.at[0])

    def o_copy(blk):
        return pltpu.make_async_copy(acc_ref, o_hbm.at[rows_of(blk), :], iosem.at[1])

    def hn_copy(blk):
        return pltpu.make_async_copy(xn_ref, hn_hbm.at[rows_of(blk), :], iosem.at[2])

    def w_copies(f, slot):
        return (
            pltpu.make_async_copy(win_hbm.at[:, pl.ds(f * bf, bf)], wa_buf.at[slot], wsem.at[0, slot]),
            pltpu.make_async_copy(win_hbm.at[:, pl.ds((nf + f) * bf, bf)], wb_buf.at[slot], wsem.at[1, slot]),
            pltpu.make_async_copy(wout_hbm.at[pl.ds(f * bf, bf), :], wo_buf.at[slot], wsem.at[2, slot]),
        )

    def start(f, slot):
        for c in w_copies(f, slot):
            c.start()

    def wait(f, slot):
        for c in w_copies(f, slot):
            c.wait()

    start(0, 0)

    @pl.when(i == 0)
    def _():
        x_copy(0).start()

    x_copy(i).wait()

    if mode == "next_norm":
        @pl.when(i > 0)
        def _():
            hn_copy(i - 1).wait()

    def normalise(rows):
        xn_ref[rows, :] = _rms_rows(stage_ref[rows, :], g_ref[...]).astype(BF16)

    _row_chunks(bm, 64, normalise)

    @pl.when(i > 0)
    def _():
        o_copy(i - 1).wait()

    def init(rows):
        acc_ref[rows, :] = 2.0 * stage_ref[rows, :]

    _row_chunks(bm, 64, init)

    @pl.when(i + 1 < n_blocks)
    def _():
        x_copy(i + 1).start()

    def tile(slot):
        xn = xn_ref[...]
        a = jnp.dot(xn, wa_buf[slot], preferred_element_type=F32)
        b = jnp.dot(xn, wb_buf[slot], preferred_element_type=F32)
        gated = (a * jax.nn.sigmoid(a) * b).astype(BF16)
        bn = min(d, 2 * V7X_MXU_DIM)
        for n0 in range(0, d, bn):
            acc_ref[:, n0:n0 + bn] += jnp.dot(gated, wo_buf[slot, :, n0:n0 + bn], preferred_element_type=F32)

    n_pairs = (nf - 1) // 2

    def pair(p, carry):
        f0 = 2 * p
        start(f0 + 1, 1)
        wait(f0, 0)
        tile(0)
        start(f0 + 2, 0)
        wait(f0 + 1, 1)
        tile(1)
        return carry

    lax.fori_loop(0, n_pairs, pair, 0)
    f_last = 2 * n_pairs
    if nf - f_last == 2:
        start(f_last + 1, 1)
    wait(f_last, 0)
    tile(0)
    if nf - f_last == 2:
        wait(f_last + 1, 1)
        tile(1)

    def finish(rows):
        h = 0.5 * acc_ref[rows, :]
        if mode == "final_norm":
            h = _rms_rows(h, g2_ref[...])
        elif mode == "next_norm":
            xn_ref[rows, :] = _rms_rows(h, g2_ref[...]).astype(BF16)
        acc_ref[rows, :] = h

    _row_chunks(bm, 64, finish)
    o_copy(i).start()
    if mode == "next_norm":
        hn_copy(i).start()

    @pl.when(i == n_blocks - 1)
    def _():
        o_copy(i).wait()
        if mode == "next_norm":
            hn_copy(i).wait()


def _ffn_big(x, g, w_in, w_out, g2, *, mode, name):
    t, d = x.shape
    f_dim = w_out.shape[0]
    bf = _blk(f_dim, V7X_MXU_DIM)
    nf = f_dim // bf
    bm = _blk(t, 1024)
    vec_spec = pl.BlockSpec((1, d), lambda i: (0, 0))
    hbm = pl.BlockSpec(memory_space=pl.ANY)
    two = mode == "next_norm"
    return pl.pallas_call(
        functools.partial(_ffn_big_kernel, mode=mode, nf=nf, bf=bf),
        grid=(t // bm,),
        in_specs=[vec_spec, vec_spec, hbm, hbm, hbm],
        out_specs=[hbm] + [hbm] * two,
        out_shape=[jax.ShapeDtypeStruct((t, d), F32)] + [jax.ShapeDtypeStruct((t, d), BF16)] * two,
        scratch_shapes=[
            pltpu.VMEM((bm, d), F32),
            pltpu.VMEM((bm, d), F32),
            pltpu.VMEM((bm, d), BF16),
            pltpu.VMEM((2, d, bf), BF16),
            pltpu.VMEM((2, d, bf), BF16),
            pltpu.VMEM((2, bf, d), BF16),
            pltpu.SemaphoreType.DMA((3, 2)),
            pltpu.SemaphoreType.DMA((3,)),
        ],
        compiler_params=_params(("arbitrary",)),
        name=name,
    )(g.reshape(1, d), g2.reshape(1, d), x, w_in.astype(BF16), w_out.astype(BF16))


def _ffn_loop_kernel(x_ref, g_ref, g2_ref, win_hbm, wout_hbm, o_ref, xn_ref, wa_buf, wb_buf, wo_buf, sem,
                     *, mode, nf, bf):
    bm, d = x_ref.shape

    def copies_in(f, slot):
        return (
            pltpu.make_async_copy(win_hbm.at[:, pl.ds(f * bf, bf)], wa_buf.at[slot], sem.at[0, slot]),
            pltpu.make_async_copy(win_hbm.at[:, pl.ds((nf + f) * bf, bf)], wb_buf.at[slot], sem.at[1, slot]),
        )

    def copies_out(f, slot):
        return (pltpu.make_async_copy(wout_hbm.at[pl.ds(f * bf, bf), :], wo_buf.at[pl.ds(slot * bf, bf), :],
                                      sem.at[2, slot]),)

    def start(copies):
        for c in copies:
            c.start()

    def wait(copies):
        for c in copies:
            c.wait()

    start(copies_in(0, 0))
    start(copies_out(0, 0))

    def prologue(rows):
        xn_ref[rows, :] = _rms_rows(x_ref[rows, :], g_ref[...]).astype(BF16)
        o_ref[rows, :] = jnp.zeros((rows.size, d), F32)

    _row_chunks(bm, 64, prologue)

    def gate(slot):
        xn = xn_ref[...]
        a = jnp.dot(xn, wa_buf[slot], preferred_element_type=F32)
        b = jnp.dot(xn, wb_buf[slot], preferred_element_type=F32)
        return (a * jax.nn.sigmoid(a) * b).astype(BF16)

    def project(gated, w_rows):
        bn = min(d, 1024)
        for n0 in range(0, d, bn):
            o_ref[:, n0:n0 + bn] += jnp.dot(gated, wo_buf[w_rows, n0:n0 + bn], preferred_element_type=F32)

    def two_tiles(f0, more):
        start(copies_in(f0 + 1, 1))
        start(copies_out(f0 + 1, 1))
        wait(copies_in(f0, 0))
        g0 = gate(0)
        if more:
            start(copies_in(f0 + 2, 0))
        wait(copies_in(f0 + 1, 1))
        g1 = gate(1)
        wait(copies_out(f0, 0))
        wait(copies_out(f0 + 1, 1))
        project(jnp.concatenate([g0, g1], axis=1), slice(0, 2 * bf))
        if more:
            start(copies_out(f0 + 2, 0))

    n_pairs = (nf - 1) // 2

    def pair(p, carry):
        two_tiles(2 * p, True)
        return carry

    lax.fori_loop(0, n_pairs, pair, 0)
    f_last = 2 * n_pairs
    if nf - f_last == 2:
        two_tiles(f_last, False)
    else:
        wait(copies_in(f_last, 0))
        g0 = gate(0)
        wait(copies_out(f_last, 0))
        project(g0, slice(0, bf))

    def epilogue(rows):
        h = x_ref[rows, :] + 0.5 * o_ref[rows, :]
        if mode == "final_norm":
            h = _rms_rows(h, g2_ref[...])
        o_ref[rows, :] = h

    _row_chunks(bm, 64, epilogue)


def _ffn_loop(x, g, w_in, w_out, g2, *, mode, name):
    t, d = x.shape
    f_dim = w_out.shape[0]
    bf = _blk(f_dim, V7X_MXU_DIM)
    nf = f_dim // bf
    bm = _blk(t, 512)
    row_spec = pl.BlockSpec((bm, d), lambda i: (i, 0))
    vec_spec = pl.BlockSpec((1, d), lambda i: (0, 0))
    return pl.pallas_call(
        functools.partial(_ffn_loop_kernel, mode=mode, nf=nf, bf=bf),
        grid=(t // bm,),
        in_specs=[row_spec, vec_spec, vec_spec,
                  pl.BlockSpec(memory_space=pl.ANY), pl.BlockSpec(memory_space=pl.ANY)],
        out_specs=row_spec,
        out_shape=jax.ShapeDtypeStruct((t, d), F32),
        scratch_shapes=[
            pltpu.VMEM((bm, d), BF16),
            pltpu.VMEM((2, d, bf), BF16),
            pltpu.VMEM((2, d, bf), BF16),
            pltpu.VMEM((2 * bf, d), BF16),
            pltpu.SemaphoreType.DMA((3, 2)),
        ],
        compiler_params=_params(("arbitrary",)),
        name=name,
    )(x, g.reshape(1, d), g2.reshape(1, d), w_in.astype(BF16), w_out.astype(BF16))


def _mm_kernel(a_ref, w_ref, o_ref, *, side_cast):
    o_ref[...] = jnp.dot(a_ref[...], w_ref[...], preferred_element_type=F32).astype(o_ref.dtype)
    side_cast()


def _mm(a, w, *, name, bm=1024, bn=1024, side=()):
    m, k = a.shape
    n = w.shape[1]
    bm = _blk(m, bm)
    bn = _blk(n, bn)
    return _call(
        _mm_kernel,
        grid=(m // bm, n // bn),
        in_specs=[
            pl.BlockSpec((bm, k), lambda i, j: (i, 0)),
            pl.BlockSpec((k, bn), lambda i, j: (0, j)),
        ],
        out_specs=[pl.BlockSpec((bm, bn), lambda i, j: (i, j))],
        out_shapes=[jax.ShapeDtypeStruct((m, n), BF16)],
        args=(a, w.astype(BF16)),
        side=side,
        name=name,
    )


def _norm_mm_kernel(a_ref, g_ref, *rest, n_w):
    w_refs, o_refs, an_ref = rest[:n_w], rest[n_w:2 * n_w], rest[2 * n_w]

    @pl.when(pl.program_id(1) == 0)
    def _():
        def body(rows):
            an_ref[rows, :] = _rms_rows(a_ref[rows, :].astype(F32), g_ref[...]).astype(BF16)

        _row_chunks(a_ref.shape[0], 64, body)

    for w_ref, o_ref in zip(w_refs, o_refs):
        o_ref[...] = jnp.dot(an_ref[...], w_ref[...], preferred_element_type=F32).astype(o_ref.dtype)


def _norm_mm(a, g, ws, *, name, bm=512, bn=1024):
    m, k = a.shape
    n = ws[0].shape[1]
    bm = _blk(m, bm)
    bn = _blk(n, bn)
    tile = pl.BlockSpec((bm, bn), lambda i, j: (i, j))
    return pl.pallas_call(
        functools.partial(_norm_mm_kernel, n_w=len(ws)),
        grid=(m // bm, n // bn),
        in_specs=[
            pl.BlockSpec((bm, k), lambda i, j: (i, 0)),
            pl.BlockSpec((1, k), lambda i, j: (0, 0)),
        ] + [pl.BlockSpec((k, bn), lambda i, j: (0, j))] * len(ws),
        out_specs=[tile] * len(ws),
        out_shape=[jax.ShapeDtypeStruct((m, n), BF16)] * len(ws),
        scratch_shapes=[pltpu.VMEM((bm, k), BF16)],
        compiler_params=_params(("arbitrary", "arbitrary")),
        name=name,
    )(a, g.reshape(1, k), *[w.astype(BF16) for w in ws])


def _mm_res_kernel(a_ref, w_ref, res_ref, o_ref, *copy_ref):
    bn = o_ref.shape[1]
    cn = min(bn, 2 * V7X_MXU_DIM)
    for n0 in range(0, bn, cn):
        cols = slice(n0, n0 + cn)
        out = res_ref[:, cols] + jnp.dot(a_ref[...], w_ref[:, cols], preferred_element_type=F32)
        o_ref[:, cols] = out
        for ref in copy_ref:
            ref[:, cols] = out.astype(ref.dtype)


def _mm_res(a, w, res, *, name, bm=1024, bn=1024, bf16_copy=False):
    m, k = a.shape
    n = w.shape[1]
    bm = _blk(m, bm)
    bn = _blk(n, bn)
    tile = pl.BlockSpec((bm, bn), lambda i, j: (i, j))
    return pl.pallas_call(
        _mm_res_kernel,
        grid=(m // bm, n // bn),
        in_specs=[
            pl.BlockSpec((bm, k), lambda i, j: (i, 0)),
            pl.BlockSpec((k, bn), lambda i, j: (0, j)),
            tile,
        ],
        out_specs=[tile] + [tile] * bf16_copy,
        out_shape=[jax.ShapeDtypeStruct((m, n), F32)] + [jax.ShapeDtypeStruct((m, n), BF16)] * bf16_copy,
        compiler_params=_params(("parallel", "arbitrary")),
        name=name,
    )(a, w.astype(BF16), res)


def _cmul(ar, ai, br, bi):
    return ar * br - ai * bi, ar * bi + ai * br


def _gelu_exact(x):
    return 0.5 * x * (1.0 + lax.erf(x * (2.0 ** -0.5)))


def _ssm_kernel(tok_ref, are_ref, aim_ref, ldt_ref, braw_ref, craw_ref, dsk_ref, o_ref,
                bstack_ref, nst_ref, dstack_ref, pw_ref, apow_ref, sre_ref, sim_ref, w_ref,
                slab_ref, u_ref, *, n_seq, n_chunks, n_log):
    w256 = V7X_MXU_DIM
    ns = braw_ref.shape[2] // 2
    n_strip = ns // V7X_LANES
    off = sre_ref.shape[2] - n_chunks
    rows = n_seq * n_chunks
    n_slab = w256 // V7X_LANES

    for hf in range(n_slab):
        slab_ref[hf] = tok_ref[:, hf * V7X_LANES:(hf + 1) * V7X_LANES].astype(F32)
    for r in range(SSM_CHUNK):
        parts = [slab_ref[hf, pl.ds(r, rows, stride=SSM_CHUNK), :] for hf in range(n_slab)]
        u_ref[r] = jnp.concatenate(parts, axis=1).astype(BF16)

    @pl.when(pl.program_id(1) == 0)
    def _prepare():
        ar = are_ref[0]
        ai = aim_ref[0]
        dt = jnp.exp(ldt_ref[0])
        mag = jnp.exp(ar * dt)
        lr = mag * jnp.cos(ai * dt)
        li = mag * jnp.sin(ai * dt)
        den = ar * ar + ai * ai
        zr = ((lr - 1.0) * ar + li * ai) / den
        zi = (li * ar - (lr - 1.0) * ai) / den
        sr, si = _cmul(braw_ref[0, :, :ns], braw_ref[0, :, ns:], zr, zi)
        cr = craw_ref[0, :, :ns]
        ci = craw_ref[0, :, ns:]
        cn = jnp.concatenate([cr, -ci], axis=1).astype(BF16)
        for j in range(SSM_CHUNK):
            bs = jnp.concatenate([sr, si], axis=1).astype(BF16)
            dj = lax.dot_general(bs, cn, (((1,), (1,)), ((), ())), preferred_element_type=F32)
            dstack_ref[(SSM_CHUNK - 1 - j) * w256:(SSM_CHUNK - j) * w256, :] = dj.astype(BF16)
            if j < SSM_SUB:
                bstack_ref[(SSM_SUB - 1 - j) * w256:(SSM_SUB - j) * w256, :] = bs
            sr, si = _cmul(sr, si, lr, li)
        mr, mi = lr, li
        for s in range(SSM_SUB):
            er, ei = _cmul(cr, ci, mr, mi)
            nst_ref[s * w256:(s + 1) * w256, :] = jnp.concatenate([er, -ei], axis=1).astype(BF16)
            if s < SSM_SUB - 1:
                mr, mi = _cmul(mr, mi, lr, li)
        pw_ref[0:1, :] = mr
        pw_ref[1:2, :] = mi
        p8r, p8i = _cmul(mr, mi, mr, mi)
        kr, ki = _cmul(p8r, p8i, p8r, p8i)
        for k in range(n_log):
            for s in range(n_strip):
                apow_ref[k, s, 0:1, :] = kr[:, s * V7X_LANES:(s + 1) * V7X_LANES]
                apow_ref[k, s, 1:2, :] = ki[:, s * V7X_LANES:(s + 1) * V7X_LANES]
            kr, ki = _cmul(kr, ki, kr, ki)
        zeros = jnp.zeros((off, V7X_LANES), F32)
        for s in range(n_strip):
            for q in range(n_seq):
                sre_ref[s, q, 0:off, :] = zeros
                sim_ref[s, q, 0:off, :] = zeros

    p4r = pw_ref[0:1, :]
    p4i = pw_ref[1:2, :]

    xr = xi = None
    for q in range(SSM_SUB):
        z = None
        for s in range(SSM_SUB):
            d = jnp.dot(u_ref[SSM_SUB * q + s], bstack_ref[s * w256:(s + 1) * w256, :],
                        preferred_element_type=F32)
            z = d if z is None else z + d
        if xr is None:
            xr, xi = z[:, :ns], z[:, ns:]
        else:
            xr, xi = _cmul(xr, xi, p4r, p4i)
            xr, xi = xr + z[:, :ns], xi + z[:, ns:]

    for s in range(n_strip):
        lanes = slice(s * V7X_LANES, (s + 1) * V7X_LANES)
        for q in range(n_seq):
            sre_ref[s, q, off:, :] = xr[q * n_chunks:(q + 1) * n_chunks, lanes]
            sim_ref[s, q, off:, :] = xi[q * n_chunks:(q + 1) * n_chunks, lanes]
    for s in range(n_strip):
        for q in range(n_seq):
            for k in range(n_log):
                sh = 1 << k
                kr = apow_ref[k, s, 0:1, :]
                ki = apow_ref[k, s, 1:2, :]
                pr = sre_ref[s, q, off - sh:off - sh + n_chunks, :]
                pi = sim_ref[s, q, off - sh:off - sh + n_chunks, :]
                tr, ti = _cmul(pr, pi, kr, ki)
                sre_ref[s, q, off:, :] = sre_ref[s, q, off:, :] + tr
                sim_ref[s, q, off:, :] = sim_ref[s, q, off:, :] + ti

    ysts = []
    for q in range(SSM_SUB):
        for s in range(n_strip):
            lanes_r = slice(s * V7X_LANES, (s + 1) * V7X_LANES)
            lanes_i = slice(ns + s * V7X_LANES, ns + (s + 1) * V7X_LANES)
            for sq in range(n_seq):
                rs = slice(sq * n_chunks, (sq + 1) * n_chunks)
                if q == 0:
                    vr = sre_ref[s, sq, off - 1:off - 1 + n_chunks, :]
                    vi = sim_ref[s, sq, off - 1:off - 1 + n_chunks, :]
                else:
                    vr, vi = _cmul(sre_ref[s, sq, off:, :], sim_ref[s, sq, off:, :],
                                   p4r[:, lanes_r], p4i[:, lanes_r])
                if q < SSM_SUB - 1:
                    sre_ref[s, sq, off:, :] = vr
                    sim_ref[s, sq, off:, :] = vi
                w_ref[rs, lanes_r] = vr.astype(BF16)
                w_ref[rs, lanes_i] = vi.astype(BF16)
        ysts.append(lax.dot_general(w_ref[...], nst_ref[...], (((1,), (1,)), ((), ())),
                                    preferred_element_type=F32))

    dsk = dsk_ref[0]
    for t in range(SSM_CHUNK):
        y = ysts[t // SSM_SUB][:, (t % SSM_SUB) * w256:(t % SSM_SUB + 1) * w256]
        y = y + dsk * u_ref[t].astype(F32)
        for r in range(t + 1):
            blk = SSM_CHUNK - 1 - t + r
            y = y + jnp.dot(u_ref[r], dstack_ref[blk * w256:(blk + 1) * w256, :],
                            preferred_element_type=F32)
        y = _gelu_exact(y)
        for hf in range(n_slab):
            slab_ref[hf, pl.ds(t, rows, stride=SSM_CHUNK), :] = y[:, hf * V7X_LANES:(hf + 1) * V7X_LANES]
    for hf in range(n_slab):
        o_ref[:, hf * V7X_LANES:(hf + 1) * V7X_LANES] = slab_ref[hf].astype(o_ref.dtype)


def _ssm(proj, a_re, a_im, log_dt, b_re, b_im, c_re, c_im, d_skip, *, n_chunks, name):
    n_tok = proj.shape[0]
    d_ssm = d_skip.shape[0]
    n_rows = n_tok // SSM_CHUNK
    n_groups, n_state = a_re.shape
    h = d_ssm // n_groups
    gpb = SSM_GROUPS_PER_BLOCK
    assert gpb * h == V7X_MXU_DIM and n_groups % gpb == 0
    n_blocks = n_groups // gpb
    ns = gpb * n_state
    n_seq = 1
    n_log = max(1, (n_chunks - 1).bit_length())
    off = max(8, 1 << (n_log - 1))

    col_group = (jnp.arange(2 * ns) % ns) // n_state
    same_group = (jnp.arange(gpb)[:, None] == col_group[None, :]).astype(F32)

    def block_diag(x_re, x_im):
        lanes = lambda x: x.reshape(n_blocks, gpb, h, n_state).transpose(0, 2, 1, 3).reshape(n_blocks, h, ns)
        x = jnp.concatenate([lanes(x_re), lanes(x_im)], axis=2)
        x = x[:, None, :, :] * same_group[None, :, None, :]
        return x.reshape(n_blocks, gpb * h, 2 * ns)

    braw = block_diag(b_re.transpose(0, 2, 1), b_im.transpose(0, 2, 1))
    craw = block_diag(c_re, c_im)
    vec = lambda x: x.reshape(n_blocks, 1, ns)
    ldt = jnp.broadcast_to(log_dt[:, None], (n_groups, n_state))
    rows = n_seq * n_chunks
    kernel = functools.partial(_ssm_kernel, n_seq=n_seq, n_chunks=n_chunks, n_log=n_log)
    vspec = pl.BlockSpec((1, 1, ns), lambda g, r: (g, 0, 0))
    mspec = pl.BlockSpec((1, V7X_MXU_DIM, 2 * ns), lambda g, r: (g, 0, 0))
    uspec = pl.BlockSpec((rows * SSM_CHUNK, V7X_MXU_DIM), lambda g, r: (r, g))
    return pl.pallas_call(
        kernel,
        grid=(n_blocks, n_rows // rows),
        in_specs=[uspec, vspec, vspec, vspec, mspec, mspec,
                  pl.BlockSpec((1, 1, V7X_MXU_DIM), lambda g, r: (g, 0, 0))],
        out_specs=uspec,
        out_shape=jax.ShapeDtypeStruct((n_tok, d_ssm), BF16),
        scratch_shapes=[
            pltpu.VMEM((SSM_SUB * V7X_MXU_DIM, 2 * ns), BF16),
            pltpu.VMEM((SSM_SUB * V7X_MXU_DIM, 2 * ns), BF16),
            pltpu.VMEM((SSM_CHUNK * V7X_MXU_DIM, V7X_MXU_DIM), BF16),
            pltpu.VMEM((8, ns), F32),
            pltpu.VMEM((n_log, ns // V7X_LANES, 8, V7X_LANES), F32),
            pltpu.VMEM((ns // V7X_LANES, n_seq, off + n_chunks, V7X_LANES), F32),
            pltpu.VMEM((ns // V7X_LANES, n_seq, off + n_chunks, V7X_LANES), F32),
            pltpu.VMEM((rows, 2 * ns), BF16),
            pltpu.VMEM((V7X_MXU_DIM // V7X_LANES, rows * SSM_CHUNK, V7X_LANES), F32),
            pltpu.VMEM((SSM_CHUNK, rows, V7X_MXU_DIM), BF16),
        ],
        compiler_params=_params(("parallel", "arbitrary")),
        name=name,
    )(proj, vec(a_re), vec(a_im), vec(ldt), braw, craw, d_skip.reshape(n_blocks, 1, V7X_MXU_DIM))


def _conv_kernel(cb_ref, cc_ref, ch_ref, w_ref, o_ref, z_ref, *, chunk):
    seq = cb_ref.shape[0]
    pad = z_ref.shape[0] - seq
    kw = w_ref.shape[0]
    z_ref[0:pad, :] = jnp.zeros((pad, z_ref.shape[1]), F32)

    def fill(rows):
        z_ref[pl.ds(rows.start + pad, rows.size), :] = cc_ref[rows, :].astype(F32) * ch_ref[rows, :].astype(F32)

    _row_chunks(seq, chunk, fill)
    for c in range(seq // chunk):
        r0 = c * chunk
        acc = None
        for k in range(kw):
            lag = kw - 1 - k
            term = w_ref[k:k + 1, :] * z_ref[pad + r0 - lag:pad + r0 - lag + chunk, :]
            acc = term if acc is None else acc + term
        o_ref[r0:r0 + chunk, :] = (cb_ref[r0:r0 + chunk, :].astype(F32) * acc).astype(o_ref.dtype)


def _conv(proj, conv_w, *, batch, seq, d_conv, col0, name):
    bc = _blk(d_conv, V7X_MXU_DIM)
    nb = d_conv // bc
    chunk = min(seq, 512)
    kw = conv_w.shape[0]
    cspec = lambda k: pl.BlockSpec((seq, bc), lambda b, j: (b, (col0 + k * d_conv) // bc + j))
    return pl.pallas_call(
        functools.partial(_conv_kernel, chunk=chunk),
        grid=(batch, nb),
        in_specs=[cspec(0), cspec(1), cspec(2), pl.BlockSpec((kw, bc), lambda b, j: (0, j))],
        out_specs=pl.BlockSpec((seq, bc), lambda b, j: (b, j)),
        out_shape=jax.ShapeDtypeStruct((batch * seq, d_conv), BF16),
        scratch_shapes=[pltpu.VMEM((8 + seq, bc), F32)],
        compiler_params=_params(("parallel", "parallel")),
        name=name,
    )(proj, proj, proj, conv_w)


def _merge_kernel(ys_ref, cz_ref, wv_ref, wg_ref, wc_ref, ga_ref, gb_ref, o_ref, *, side_cast):
    side_cast()
    ys = ys_ref[...]
    val = jnp.dot(ys, wv_ref[...], preferred_element_type=F32)
    gate = jnp.dot(ys, wg_ref[...], preferred_element_type=F32)
    yb = jnp.dot(cz_ref[...], wc_ref[...], preferred_element_type=F32)
    ya = val * jax.nn.sigmoid(gate)
    out = jax.nn.sigmoid(ga_ref[...].astype(F32)) * ya + jax.nn.sigmoid(gb_ref[...].astype(F32)) * yb
    o_ref[...] = out.astype(o_ref.dtype)


def _merge(ys, cz, glu_w, conv_w_out, proj, *, d_model, gate_col0, name, bm=1024, bn=512, side=()):
    t, k = ys.shape
    bm = _blk(t, bm)
    bn = _blk(d_model, bn)
    nj = d_model // bn
    aspec = pl.BlockSpec((bm, k), lambda i, j: (i, 0))
    wspec = lambda off: pl.BlockSpec((k, bn), lambda i, j: (0, off + j))
    gspec = lambda off: pl.BlockSpec((bm, bn), lambda i, j: (i, off + j))
    glu_w = glu_w.astype(BF16)
    return _call(
        _merge_kernel,
        grid=(t // bm, nj),
        in_specs=[aspec, aspec, wspec(0), wspec(nj), wspec(0),
                  gspec(gate_col0 // bn), gspec(gate_col0 // bn + nj)],
        out_specs=[pl.BlockSpec((bm, bn), lambda i, j: (i, j))],
        out_shapes=[jax.ShapeDtypeStruct((t, d_model), BF16)],
        args=(ys, cz, glu_w, glu_w, conv_w_out.astype(BF16), proj, proj),
        side=side,
        name=name,
    )


def _attend(q, k, v, scale):
    s = lax.dot_general(q.astype(BF16), k, (((1,), (1,)), ((), ())), preferred_element_type=F32) * scale
    p = jnp.exp(s - jnp.max(s, axis=-1, keepdims=True))
    denom = jnp.sum(p, axis=-1, keepdims=True)
    return jnp.dot(p.astype(BF16), v, preferred_element_type=F32) / denom


def _q_attn_kernel(a_ref, g_ref, wq_ref, k_ref, v_ref, o_ref, an_ref, *, scale, row_chains):
    @pl.when(pl.program_id(1) == 0)
    def _():
        def body(rows):
            an_ref[rows, :] = _rms_rows(a_ref[rows, :].astype(F32), g_ref[...]).astype(BF16)

        _row_chunks(a_ref.shape[0], 64, body)

    rc = a_ref.shape[0] // row_chains
    for r0 in range(0, a_ref.shape[0], rc):
        q = jnp.dot(an_ref[r0:r0 + rc, :], wq_ref[...], preferred_element_type=F32)
        o_ref[r0:r0 + rc, :] = _attend(q, k_ref[...], v_ref[...], scale).astype(o_ref.dtype)


def _q_attn(a, g, wq, k, v, *, batch, seq, n_mem, n_heads, name, bm=1024, row_chains=2):
    t, d = a.shape
    dh = d // n_heads
    bm = _blk(seq, bm)
    per_seq = seq // bm
    return pl.pallas_call(
        functools.partial(_q_attn_kernel, scale=dh ** -0.5, row_chains=row_chains),
        grid=(t // bm, n_heads),
        in_specs=[
            pl.BlockSpec((bm, d), lambda i, h: (i, 0)),
            pl.BlockSpec((1, d), lambda i, h: (0, 0)),
            pl.BlockSpec((d, dh), lambda i, h: (0, h)),
            pl.BlockSpec((n_mem, dh), lambda i, h: (i // per_seq, h)),
            pl.BlockSpec((n_mem, dh), lambda i, h: (i // per_seq, h)),
        ],
        out_specs=pl.BlockSpec((bm, dh), lambda i, h: (i, h)),
        out_shape=jax.ShapeDtypeStruct((t, d), BF16),
        scratch_shapes=[pltpu.VMEM((bm, d), BF16)],
        compiler_params=_params(("arbitrary", "arbitrary")),
        name=name,
    )(a, g.reshape(1, d), wq.astype(BF16), k, v)


def kernel(x, mem, ffn1_norm, ffn1_w_in, ffn1_w_out, mix_norm, mix_w_in, ssm_a_re, ssm_a_im, ssm_log_dt, ssm_b_re, ssm_b_im, ssm_c_re, ssm_c_im, ssm_d, ssm_glu_w, conv_w, conv_w_out, mix_w_out, xattn_norm, mem_norm, xattn_wq, xattn_wk, xattn_wv, xattn_wo, ffn2_norm, ffn2_w_in, ffn2_w_out, final_norm):
    batch, seq, d_model = x.shape
    n_mem = mem.shape[1]
    depth = ffn1_norm.shape[0]
    d_ssm = ssm_d.shape[1]
    d_conv = conv_w.shape[2]
    n_heads = 4
    t = batch * seq
    assert seq % SSM_CHUNK == 0
    n_chunks = seq // SSM_CHUNK

    h = x.reshape(t, d_model)
    memf = mem.reshape(batch * n_mem, d_model)
    for l in range(depth):
        last = l == depth - 1
        h, un = _ffn_big(h, ffn1_norm[l], ffn1_w_in[l], ffn1_w_out[l], mix_norm[l], mode="next_norm", name="ffn1")

        (proj,), (ffn2_w_in_b, ffn2_w_out_b, glu_w_b, conv_w_out_b, mix_w_out_b) = _mm(
            un, mix_w_in[l], name="mix_in", bm=2048, bn=512,
            side=(ffn2_w_in[l], ffn2_w_out[l], ssm_glu_w[l], conv_w_out[l], mix_w_out[l]))

        ys = _ssm(proj, ssm_a_re[l], ssm_a_im[l], ssm_log_dt[l], ssm_b_re[l], ssm_b_im[l],
                  ssm_c_re[l], ssm_c_im[l], ssm_d[l], n_chunks=n_chunks, name="ssm")

        cz = _conv(proj, conv_w[l], batch=batch, seq=seq, d_conv=d_conv, col0=d_ssm, name="conv")

        (merged,), (wq_b, wk_b, wv_b, wo_b) = _merge(
            ys, cz, glu_w_b, conv_w_out_b, proj, d_model=d_model, gate_col0=d_ssm + 3 * d_conv, name="merge",
            side=(xattn_wq[l], xattn_wk[l], xattn_wv[l], xattn_wo[l]))
        h, h_b = _mm_res(merged, mix_w_out_b, h, name="mix_out", bf16_copy=True)

        k, v = _norm_mm(memf, mem_norm[l], (wk_b, wv_b), name="xattn_kv", bn=512)
        o = _q_attn(h_b, xattn_norm[l], wq_b, k, v, batch=batch, seq=seq, n_mem=n_mem, n_heads=n_heads,
                    name="xattn")
        (h,) = _mm_res(o, wo_b, h, name="xattn_o")

        (h,) = _ffn_big(h, ffn2_norm[l], ffn2_w_in_b, ffn2_w_out_b, final_norm,
                        mode="final_norm" if last else "plain", name="ffn2")
    if depth == 0:
        raise NotImplementedError("depth 0")
    return h.reshape(batch, seq, d_model)
```
